```python
import math
import jax, jax.numpy as jnp
from jax import lax
import numpy as np

D_MODEL = 1024
BATCH = 4
SEQ = 4096
DEPTH = 1

MEM_LEN = 256
DA_HEADS = 8
DA_HEAD_DIM = 64
DA_QK = DA_HEADS * 2 * DA_HEAD_DIM
DA_V = DA_HEADS * 2 * DA_HEAD_DIM
Q_BLOCK = 128
SSD_EXPAND = 2
SSD_INNER = SSD_EXPAND * D_MODEL
SSD_HEAD_DIM = 64
SSD_HEADS = SSD_INNER // SSD_HEAD_DIM
SSD_GROUPS = 4
SSD_HEADS_PER_GROUP = SSD_HEADS // SSD_GROUPS
SSD_STATE = 128
SSD_CONV = 4
SSD_CHUNK = 128
SSD_CONV_DIM = SSD_INNER + 2 * SSD_GROUPS * SSD_STATE
XA_HEADS = 4
XA_HEAD_DIM = D_MODEL // XA_HEADS
D_FF = 2816
N_BRANCH = 2
IN_SIZES = (DA_QK, DA_QK, DA_V, SSD_INNER, SSD_CONV_DIM, SSD_HEADS, N_BRANCH * D_MODEL)
IN_WIDTH = DA_QK + DA_QK + DA_V + SSD_INNER + SSD_CONV_DIM + SSD_HEADS + N_BRANCH * D_MODEL
NORM_EPS = 1e-6
SUBLN_EPS = 1e-5

kernel_name = "hybrid_diffattn_ssd_gated_macaron"


def _rmsnorm(x, g, eps=NORM_EPS):
    xf = x.astype(jnp.float32)
    y = xf * lax.rsqrt(jnp.mean(xf * xf, axis=-1, keepdims=True) + eps)
    return (y * g.astype(jnp.float32)).astype(x.dtype)


def _swiglu(h, w_gu, w_down):
    g, u = jnp.split(h @ w_gu, 2, axis=-1)
    return (jax.nn.silu(g) * u) @ w_down


def _diff_attention(q, k, v, lam):
    b, s = q.shape[0], q.shape[1]
    q1, q2 = q[:, :, :, 0].transpose(0, 2, 1, 3), q[:, :, :, 1].transpose(0, 2, 1, 3)
    k1, k2 = k[:, :, :, 0].transpose(0, 2, 1, 3), k[:, :, :, 1].transpose(0, 2, 1, 3)
    vh = v.transpose(0, 2, 1, 3)
    nblk = s // Q_BLOCK
    scale = DA_HEAD_DIM ** -0.5
    key_pos = jnp.arange(s)

    def blocks(t):
        return t.reshape(b, DA_HEADS, nblk, Q_BLOCK, DA_HEAD_DIM).transpose(2, 0, 1, 3, 4)

    def one_block(args):
        q1b, q2b, start = args
        mask = (start + jnp.arange(Q_BLOCK))[:, None] >= key_pos[None, :]

        def probs(qb, kk):
            sc = jnp.einsum('bhqd,bhkd->bhqk', qb, kk).astype(jnp.float32) * scale
            return jax.nn.softmax(jnp.where(mask, sc, -jnp.inf), axis=-1)

        p = probs(q1b, k1) - lam * probs(q2b, k2)
        return jnp.einsum('bhqk,bhke->bhqe', p.astype(vh.dtype), vh)

    out = lax.map(one_block, (blocks(q1), blocks(q2), jnp.arange(nblk, dtype=jnp.int32) * Q_BLOCK))
    return out.transpose(1, 2, 0, 3, 4).reshape(b, DA_HEADS, s, 2 * DA_HEAD_DIM)


def _segsum(a):
    t = a.shape[-1]
    cs = jnp.cumsum(a, axis=-1)
    seg = cs[..., :, None] - cs[..., None, :]
    return jnp.where(jnp.tril(jnp.ones((t, t), dtype=bool)), seg, -jnp.inf)


def _ssd_chunked(xdt, adt, bm, cm):
    b, s = xdt.shape[0], xdt.shape[1]
    c = s // SSD_CHUNK
    G, R, P, N, Lc = SSD_GROUPS, SSD_HEADS_PER_GROUP, SSD_HEAD_DIM, SSD_STATE, SSD_CHUNK
    X = xdt.astype(jnp.float32).reshape(b, c, Lc, G, R, P)
    A = adt.astype(jnp.float32).reshape(b, c, Lc, G, R).transpose(0, 3, 4, 1, 2)
    Bc = bm.astype(jnp.float32).reshape(b, c, Lc, G, N)
    Cc = cm.astype(jnp.float32).reshape(b, c, Lc, G, N)
    a_cs = jnp.cumsum(A, axis=-1)
    Lmat = jnp.exp(_segsum(A))
    cb = jnp.einsum('bclgn,bcsgn->bgcls', Cc, Bc)
    y_diag = jnp.einsum('bgrcls,bcsgrp->bclgrp', cb[:, :, None] * Lmat, X)
    decay_states = jnp.exp(a_cs[..., -1:] - a_cs)
    states = jnp.einsum('bclgn,bgrcl,bclgrp->bcgrpn', Bc, decay_states, X)
    chunk_end = jnp.pad(a_cs[..., -1], ((0, 0), (0, 0), (0, 0), (1, 0)))
    decay_chunk = jnp.exp(_segsum(chunk_end))
    states = jnp.concatenate([jnp.zeros_like(states[:, :1]), states], axis=1)
    prev_states = jnp.einsum('bgrzc,bcgrpn->bzgrpn', decay_chunk, states)[:, :-1]
    y_off = jnp.einsum('bclgn,bcgrpn,bgrcl->bclgrp', Cc, prev_states, jnp.exp(a_cs))
    return (y_diag + y_off).reshape(b, s, SSD_HEADS, P)


def _depthwise_causal_conv(u, w, bias):
    y = lax.conv_general_dilated(u, w.astype(u.dtype)[:, None, :], window_strides=(1,),
                                 padding=[(SSD_CONV - 1, 0)],
                                 dimension_numbers=('NWC', 'WIO', 'NWC'),
                                 feature_group_count=u.shape[-1])
    return y + bias.astype(u.dtype)


def _layer(x, mem, layer_idx,
           ffn1_pre_g, ffn1_post_g, ffn1_w_gu, ffn1_w_down,
           mix_pre_g, mix_post_g, w_in, b_gate,
           da_lambda_q1, da_lambda_k1, da_lambda_q2, da_lambda_k2, da_subln_g,
           ssd_conv_w, ssd_conv_b, ssd_dt_bias, ssd_A_log, ssd_D, ssd_norm_g,
           w_branch_attn, w_branch_ssd, w_mix_out,
           xa_pre_g, xa_post_g, mem_norm_g, xa_w_q, xa_w_kv, xa_w_o,
           ffn2_pre_g, ffn2_post_g, ffn2_w_gu, ffn2_w_down):
    b, s, _ = x.shape
    x = x + 0.5 * _rmsnorm(_swiglu(_rmsnorm(x, ffn1_pre_g), ffn1_w_gu, ffn1_w_down), ffn1_post_g)

    h = _rmsnorm(x, mix_pre_g)
    cuts = [sum(IN_SIZES[:i + 1]) for i in range(len(IN_SIZES) - 1)]
    q, k, v, z, xbc, dt_raw, gate_logits = jnp.split(h @ w_in, cuts, axis=-1)

    lam_init = 0.8 - 0.6 * math.exp(-0.3 * layer_idx)
    lam = (jnp.exp(jnp.sum(da_lambda_q1 * da_lambda_k1).astype(jnp.float32))
           - jnp.exp(jnp.sum(da_lambda_q2 * da_lambda_k2).astype(jnp.float32)) + lam_init)
    o = _diff_attention(q.reshape(b, s, DA_HEADS, 2, DA_HEAD_DIM),
                        k.reshape(b, s, DA_HEADS, 2, DA_HEAD_DIM),
                        v.reshape(b, s, DA_HEADS, 2 * DA_HEAD_DIM), lam)
    o = _rmsnorm(o, da_subln_g, SUBLN_EPS) * (1.0 - lam_init)
    attn_out = o.transpose(0, 2, 1, 3).reshape(b, s, DA_V) @ w_branch_attn

    xbc = jax.nn.silu(_depthwise_causal_conv(xbc, ssd_conv_w, ssd_conv_b))
    xs, bm, cm = jnp.split(xbc, [SSD_INNER, SSD_INNER + SSD_GROUPS * SSD_STATE], axis=-1)
    dt = jax.nn.softplus(dt_raw.astype(jnp.float32) + ssd_dt_bias.astype(jnp.float32))
    a = -jnp.exp(ssd_A_log.astype(jnp.float32))
    xh = xs.reshape(b, s, SSD_HEADS, SSD_HEAD_DIM)
    y = _ssd_chunked(xh * dt[..., None], a * dt,
                     bm.reshape(b, s, SSD_GROUPS, SSD_STATE), cm.reshape(b, s, SSD_GROUPS, SSD_STATE))
    y = (y + ssd_D.astype(jnp.float32)[:, None] * xh).astype(x.dtype).reshape(b, s, SSD_INNER)
    yg = (y * jax.nn.silu(z)).reshape(b, s, SSD_GROUPS, SSD_INNER // SSD_GROUPS)
    y = _rmsnorm(yg, ssd_norm_g.reshape(SSD_GROUPS, SSD_INNER // SSD_GROUPS), SUBLN_EPS)
    ssd_out = y.reshape(b, s, SSD_INNER) @ w_branch_ssd

    g_attn, g_ssd = jnp.split(jax.nn.sigmoid(gate_logits + b_gate), N_BRANCH, axis=-1)
    mixed = (g_attn * attn_out + g_ssd * ssd_out) @ w_mix_out
    x = x + _rmsnorm(mixed, mix_post_g)

    hq = _rmsnorm(x, xa_pre_g)
    qx = (hq @ xa_w_q).reshape(b, s, XA_HEADS, XA_HEAD_DIM)
    kx, vx = jnp.split(_rmsnorm(mem, mem_norm_g) @ xa_w_kv, 2, axis=-1)
    m = mem.shape[1]
    kx = kx.reshape(b, m, XA_HEADS, XA_HEAD_DIM)
    vx = vx.reshape(b, m, XA_HEADS, XA_HEAD_DIM)
    sc = jnp.einsum('bqhd,bkhd->bhqk', qx, kx).astype(jnp.float32) * (XA_HEAD_DIM ** -0.5)
    p = jax.nn.softmax(sc, axis=-1).astype(vx.dtype)
    xo = jnp.einsum('bhqk,bkhd->bqhd', p, vx).reshape(b, s, D_MODEL) @ xa_w_o
    x = x + _rmsnorm(xo, xa_post_g)

    x = x + 0.5 * _rmsnorm(_swiglu(_rmsnorm(x, ffn2_pre_g), ffn2_w_gu, ffn2_w_down), ffn2_post_g)
    return x


def setup_inputs(seed: int = 0) -> dict:
    key = jax.random.key(seed)
    ks = iter(jax.random.split(key, 48))
    L = DEPTH

    def nrm(shape, scale):
        return scale * jax.random.normal(next(ks), shape, jnp.float32)

    def gain(n):
        return 1.0 + 0.02 * jax.random.normal(next(ks), (L, n), jnp.float32)

    inp = {}
    inp["x"] = nrm((BATCH, SEQ, D_MODEL), 1.0)
    inp["mem"] = nrm((BATCH, MEM_LEN, D_MODEL), 1.0)
    inp["ffn1_pre_g"] = gain(D_MODEL)
    inp["ffn1_post_g"] = gain(D_MODEL)
    inp["ffn1_w_gu"] = nrm((L, D_MODEL, 2 * D_FF), D_MODEL ** -0.5)
    inp["ffn1_w_down"] = nrm((L, D_FF, D_MODEL), D_FF ** -0.5)
    inp["mix_pre_g"] = gain(D_MODEL)
    inp["mix_post_g"] = gain(D_MODEL)
    inp["w_in"] = nrm((L, D_MODEL, IN_WIDTH), D_MODEL ** -0.5)
    inp["b_gate"] = nrm((L, N_BRANCH * D_MODEL), 0.02)
    inp["da_lambda_q1"] = nrm((L, DA_HEAD_DIM), 0.1)
    inp["da_lambda_k1"] = nrm((L, DA_HEAD_DIM), 0.1)
    inp["da_lambda_q2"] = nrm((L, DA_HEAD_DIM), 0.1)
    inp["da_lambda_k2"] = nrm((L, DA_HEAD_DIM), 0.1)
    inp["da_subln_g"] = gain(2 * DA_HEAD_DIM)
    inp["ssd_conv_w"] = nrm((L, SSD_CONV, SSD_CONV_DIM), SSD_CONV ** -0.5)
    inp["ssd_conv_b"] = nrm((L, SSD_CONV_DIM), 0.02)
    u = jax.random.uniform(next(ks), (L, SSD_HEADS), jnp.float32)
    dt0 = jnp.exp(u * (math.log(0.1) - math.log(0.001)) + math.log(0.001))
    inp["ssd_dt_bias"] = dt0 + jnp.log(-jnp.expm1(-dt0))
    inp["ssd_A_log"] = jnp.log(jax.random.uniform(next(ks), (L, SSD_HEADS), jnp.float32, 1.0, 16.0))
    inp["ssd_D"] = 1.0 + nrm((L, SSD_HEADS), 0.1)
    inp["ssd_norm_g"] = gain(SSD_INNER)
    inp["w_branch_attn"] = nrm((L, DA_V, D_MODEL), DA_V ** -0.5)
    inp["w_branch_ssd"] = nrm((L, SSD_INNER, D_MODEL), SSD_INNER ** -0.5)
    inp["w_mix_out"] = nrm((L, D_MODEL, D_MODEL), D_MODEL ** -0.5)
    inp["xa_pre_g"] = gain(D_MODEL)
    inp["xa_post_g"] = gain(D_MODEL)
    inp["mem_norm_g"] = gain(D_MODEL)
    inp["xa_w_q"] = nrm((L, D_MODEL, D_MODEL), D_MODEL ** -0.5)
    inp["xa_w_kv"] = nrm((L, D_MODEL, 2 * D_MODEL), D_MODEL ** -0.5)
    inp["xa_w_o"] = nrm((L, D_MODEL, D_MODEL), D_MODEL ** -0.5)
    inp["ffn2_pre_g"] = gain(D_MODEL)
    inp["ffn2_post_g"] = gain(D_MODEL)
    inp["ffn2_w_gu"] = nrm((L, D_MODEL, 2 * D_FF), D_MODEL ** -0.5)
    inp["ffn2_w_down"] = nrm((L, D_FF, D_MODEL), D_FF ** -0.5)
    return inp


def reference(x, mem, ffn1_pre_g, ffn1_post_g, ffn1_w_gu, ffn1_w_down,
              mix_pre_g, mix_post_g, w_in, b_gate,
              da_lambda_q1, da_lambda_k1, da_lambda_q2, da_lambda_k2, da_subln_g,
              ssd_conv_w, ssd_conv_b, ssd_dt_bias, ssd_A_log, ssd_D, ssd_norm_g,
              w_branch_attn, w_branch_ssd, w_mix_out,
              xa_pre_g, xa_post_g, mem_norm_g, xa_w_q, xa_w_kv, xa_w_o,
              ffn2_pre_g, ffn2_post_g, ffn2_w_gu, ffn2_w_down):
    for l in range(DEPTH):
        x = _layer(x, mem, l,
                   ffn1_pre_g[l], ffn1_post_g[l], ffn1_w_gu[l], ffn1_w_down[l],
                   mix_pre_g[l], mix_post_g[l], w_in[l], b_gate[l],
                   da_lambda_q1[l], da_lambda_k1[l], da_lambda_q2[l], da_lambda_k2[l], da_subln_g[l],
                   ssd_conv_w[l], ssd_conv_b[l], ssd_dt_bias[l], ssd_A_log[l], ssd_D[l], ssd_norm_g[l],
                   w_branch_attn[l], w_branch_ssd[l], w_mix_out[l],
                   xa_pre_g[l], xa_post_g[l], mem_norm_g[l], xa_w_q[l], xa_w_kv[l], xa_w_o[l],
                   ffn2_pre_g[l], ffn2_post_g[l], ffn2_w_gu[l], ffn2_w_down[l])
    return x
```

```python
import functools
import math

import jax
import jax.numpy as jnp
from jax import lax
from jax.experimental import pallas as pl
from jax.experimental.pallas import tpu as pltpu

F32 = jnp.float32
BF16 = jnp.bfloat16

D_MODEL = 1024
MEM_LEN = 256
DA_HEADS = 8
DA_HEAD_DIM = 64
DA_PAIR = 2 * DA_HEAD_DIM
DA_QK = DA_HEADS * DA_PAIR
DA_V = DA_HEADS * DA_PAIR
SSD_INNER = 2 * D_MODEL
SSD_HEAD_DIM = 64
SSD_HEADS = SSD_INNER // SSD_HEAD_DIM
SSD_GROUPS = 4
SSD_HEADS_PER_GROUP = SSD_HEADS // SSD_GROUPS
SSD_STATE = 128
SSD_CONV = 4
SSD_CHUNK = 128
SSD_GROUP_W = SSD_INNER // SSD_GROUPS
SSD_CONV_DIM = SSD_INNER + 2 * SSD_GROUPS * SSD_STATE
XA_HEADS = 4
XA_HEAD_DIM = D_MODEL // XA_HEADS
D_FF = 2816
N_BRANCH = 2
NORM_EPS = 1e-6
SUBLN_EPS = 1e-5

LANES = 128
SUBLANES = 8
VMEM_LIMIT = 56 * 1024 * 1024

PROJ_XBC_OFF = 3 * DA_QK
PROJ_Z_OFF = PROJ_XBC_OFF + SSD_CONV_DIM
PROJ_GATE_OFF = PROJ_Z_OFF + SSD_INNER
PROJ_WIDTH = PROJ_GATE_OFF + N_BRANCH * D_MODEL

FFN_TM = 512
FFN_CHUNK = 256
PROJ_TM = 1024
PROJ_TN = 1024
ATTN_TQ = 256
MIX_TM = 512


def _rms(x, g, eps):
    return x * lax.rsqrt(jnp.mean(x * x, axis=-1, keepdims=True) + eps) * g


def _silu(x):
    return x * jax.nn.sigmoid(x)


def _dot(a, b):
    return jnp.dot(a, b, preferred_element_type=F32)


def _dot_nt(a, b):
    return lax.dot_general(a, b, (((1,), (1,)), ((), ())), preferred_element_type=F32)


def _const_spec(shape):
    nd = len(shape)
    return pl.BlockSpec(shape, lambda *_: (0,) * nd, pipeline_mode=pl.Buffered(1))


def _ffn_body(x_ref, pre_ref, wgu_ref, wd_ref, post_ref, o_ref):
    x = x_ref[...]
    hn = _rms(x, pre_ref[...], NORM_EPS).astype(BF16)
    acc = None
    for c0 in range(0, D_FF, FFN_CHUNK):
        g = _dot(hn, wgu_ref[:, c0:c0 + FFN_CHUNK])
        u = _dot(hn, wgu_ref[:, D_FF + c0:D_FF + c0 + FFN_CHUNK])
        a = (_silu(g) * u).astype(BF16)
        d = _dot(a, wd_ref[c0:c0 + FFN_CHUNK, :])
        acc = d if acc is None else acc + d
    o_ref[...] = x + 0.5 * _rms(acc, post_ref[...], NORM_EPS)


def _ffn(x, pre_g, w_gu, w_down, post_g):
    t = x.shape[0]
    return pl.pallas_call(
        _ffn_body,
        grid=(t // FFN_TM,),
        in_specs=[
            pl.BlockSpec((FFN_TM, D_MODEL), lambda i: (i, 0)),
            _const_spec((1, D_MODEL)),
            _const_spec((D_MODEL, 2 * D_FF)),
            _const_spec((D_FF, D_MODEL)),
            _const_spec((1, D_MODEL)),
        ],
        out_specs=pl.BlockSpec((FFN_TM, D_MODEL), lambda i: (i, 0)),
        out_shape=jax.ShapeDtypeStruct((t, D_MODEL), F32),
        compiler_params=pltpu.CompilerParams(
            dimension_semantics=("parallel",), vmem_limit_bytes=VMEM_LIMIT),
        name="ffn",
    )(x, pre_g, w_gu, w_down, post_g)


def _proj_body(x_ref, g_ref, w_ref, ws_ref, o_ref, os_ref, hn_ref):
    @pl.when(pl.program_id(1) == 0)
    def _():
        hn = _rms(x_ref[...], g_ref[...], NORM_EPS).astype(BF16)
        hn_ref[...] = hn
        os_ref[...] = _dot(hn, ws_ref[...])

    o_ref[...] = _dot(hn_ref[...], w_ref[...]).astype(o_ref.dtype)


def _norm_proj(x, g, w, w_side, tm, tn):
    t, n = x.shape[0], w.shape[1]
    ns = w_side.shape[1]
    return pl.pallas_call(
        _proj_body,
        grid=(t // tm, n // tn),
        in_specs=[
            pl.BlockSpec((tm, D_MODEL), lambda i, j: (i, 0)),
            pl.BlockSpec((1, D_MODEL), lambda i, j: (0, 0)),
            pl.BlockSpec((D_MODEL, tn), lambda i, j: (0, j)),
            pl.BlockSpec((D_MODEL, ns), lambda i, j: (0, 0)),
        ],
        out_specs=[
            pl.BlockSpec((tm, tn), lambda i, j: (i, j)),
            pl.BlockSpec((tm, ns), lambda i, j: (i, 0)),
        ],
        out_shape=[
            jax.ShapeDtypeStruct((t, n), BF16),
            jax.ShapeDtypeStruct((t, ns), F32),
        ],
        scratch_shapes=[pltpu.VMEM((tm, D_MODEL), BF16)],
        compiler_params=pltpu.CompilerParams(
            dimension_semantics=("parallel", "arbitrary"), vmem_limit_bytes=VMEM_LIMIT),
        name="norm_proj",
    )(x, g, w, w_side)


def _attn_body(lam_init, seq, q_ref, k_ref, v_ref, lam_ref, g_ref, o_ref,
               qq_ref, m_ref, acc_ref, vx_ref):
    tq = ATTN_TQ
    lamv = lam_ref[...]
    lam = (jnp.exp(jnp.sum(lamv[0:1] * lamv[1:2], axis=-1, keepdims=True))
           - jnp.exp(jnp.sum(lamv[2:3] * lamv[3:4], axis=-1, keepdims=True)) + lam_init)
    lane = lax.broadcasted_iota(jnp.int32, (1, DA_PAIR), 1)
    first_half = lane < DA_HEAD_DIM
    vx_ref[:, DA_PAIR:] = jnp.broadcast_to(
        jnp.where(lane == 0, 1.0, 0.0), (tq, DA_PAIR)).astype(BF16)
    row = lax.broadcasted_iota(jnp.int32, (2 * tq, tq), 0)
    col = lax.broadcasted_iota(jnp.int32, (2 * tq, tq), 1)
    causal = jnp.where(row >= tq, row - tq, row) >= col
    scale = DA_HEAD_DIM ** -0.5

    def kv_step(j, masked):
        kb = k_ref[pl.ds(pl.multiple_of(j * tq, tq), tq), :]
        vx_ref[:, :DA_PAIR] = v_ref[pl.ds(pl.multiple_of(j * tq, tq), tq), :]
        s = _dot_nt(qq_ref[...], kb)
        if masked:
            s = jnp.where(causal, s, -jnp.inf)
        m_old = m_ref[...]
        m_new = jnp.maximum(m_old, jnp.max(s, axis=-1, keepdims=True))
        p = jnp.exp(s - m_new)
        alpha = jnp.exp(m_old - m_new)
        acc_ref[...] = acc_ref[...] * alpha + _dot(p.astype(BF16), vx_ref[...])
        m_ref[...] = m_new

    def q_tile(i, carry):
        r0 = pl.multiple_of(i * tq, tq)
        q = q_ref[pl.ds(r0, tq), :].astype(F32) * scale
        qq_ref[0:tq, :] = jnp.where(first_half, q, 0.0).astype(BF16)
        qq_ref[tq:2 * tq, :] = jnp.where(first_half, 0.0, q).astype(BF16)
        m_ref[...] = jnp.full(m_ref.shape, -jnp.inf, F32)
        acc_ref[...] = jnp.zeros(acc_ref.shape, F32)

        def full_step(j, c):
            kv_step(j, False)
            return c

        lax.fori_loop(0, i, full_step, 0)
        kv_step(i, True)
        acc = acc_ref[...]
        o = acc[:, :DA_PAIR] / acc[:, DA_PAIR:DA_PAIR + 1]
        res = o[:tq] - lam * o[tq:]
        out = _rms(res, g_ref[...], SUBLN_EPS) * (1.0 - lam_init)
        o_ref[pl.ds(r0, tq), :] = out.astype(o_ref.dtype)
        return carry

    lax.fori_loop(0, seq // tq, q_tile, 0)


def _diff_attn(proj, lam_params, subln_g, lam_init, batch, seq):
    t = batch * seq
    tq = ATTN_TQ
    return pl.pallas_call(
        functools.partial(_attn_body, lam_init, seq),
        grid=(batch, DA_HEADS),
        in_specs=[
            pl.BlockSpec((seq, DA_PAIR), lambda b, h: (b, h)),
            pl.BlockSpec((seq, DA_PAIR), lambda b, h: (b, DA_HEADS + h)),
            pl.BlockSpec((seq, DA_PAIR), lambda b, h: (b, 2 * DA_HEADS + h)),
            pl.BlockSpec((4, DA_HEAD_DIM), lambda b, h: (0, 0)),
            pl.BlockSpec((1, DA_PAIR), lambda b, h: (0, 0)),
        ],
        out_specs=pl.BlockSpec((seq, DA_PAIR), lambda b, h: (b, h)),
        out_shape=jax.ShapeDtypeStruct((t, DA_V), BF16),
        scratch_shapes=[
            pltpu.VMEM((2 * tq, DA_PAIR), BF16),
            pltpu.VMEM((2 * tq, 1), F32),
            pltpu.VMEM((2 * tq, 2 * DA_PAIR), F32),
            pltpu.VMEM((tq, 2 * DA_PAIR), BF16),
        ],
        compiler_params=pltpu.CompilerParams(
            dimension_semantics=("parallel", "parallel"), vmem_limit_bytes=VMEM_LIMIT),
        name="diff_attn",
    )(proj, proj, proj, lam_params, subln_g)


def _ssd_body(xbc_ref, z_ref, dt_ref, cw_ref, cb_ref, dtb_ref, alog_ref, dexp_ref, ng_ref,
              y_ref, xpad_ref, rt_ref, ybuf_ref):
    lc = SSD_CHUNK
    hist = SUBLANES

    @pl.when(pl.program_id(1) == 0)
    def _():
        xpad_ref[0:hist, :] = jnp.zeros((hist, SSD_CONV_DIM), F32)
        rt_ref[...] = jnp.zeros(rt_ref.shape, F32)

    xpad_ref[hist:hist + lc, :] = xbc_ref[...].astype(F32)
    cw = cw_ref[...]
    conv = cb_ref[...]
    for k in range(SSD_CONV):
        off = hist - (SSD_CONV - 1) + k
        conv = conv + cw[k:k + 1, :] * xpad_ref[off:off + lc, :]
    xpad_ref[0:hist, :] = xpad_ref[lc:lc + hist, :]
    act = _silu(conv)

    dt = jax.nn.softplus(dt_ref[...] + dtb_ref[...])
    adt = -jnp.exp(alog_ref[...]) * dt
    ri = lax.broadcasted_iota(jnp.int32, (lc, lc), 0)
    ci = lax.broadcasted_iota(jnp.int32, (lc, lc), 1)
    lower = ri >= ci
    tril = jnp.where(lower, 1.0, 0.0)
    acs = jnp.dot(tril, adt, precision=lax.Precision.HIGHEST, preferred_element_type=F32)
    acs_t = acs.T
    dt_t = dt.T
    tot_t = acs_t[:, lc - 1:lc]
    w_t = jnp.exp(tot_t - acs_t) * dt_t
    etot_t = jnp.exp(tot_t)

    lane = lax.broadcasted_iota(jnp.int32, (1, LANES), 1)
    left = lane < SSD_HEAD_DIM

    for g in range(SSD_GROUPS):
        b0 = SSD_INNER + g * SSD_STATE
        c0 = SSD_INNER + SSD_GROUPS * SSD_STATE + g * SSD_STATE
        bg = act[:, b0:b0 + SSD_STATE]
        cg = act[:, c0:c0 + SSD_STATE]
        cbg = _dot_nt(cg.astype(BF16), bg.astype(BF16))
        bg_t = bg.T
        for pr in range(SSD_HEADS_PER_GROUP // 2):
            h0 = g * SSD_HEADS_PER_GROUP + 2 * pr
            x0 = h0 * SSD_HEAD_DIM
            xpair = act[:, x0:x0 + LANES]
            rtpair = rt_ref[:, x0:x0 + LANES]
            xl = jnp.where(left, xpair, 0.0).astype(BF16)
            xr = jnp.where(left, 0.0, xpair).astype(BF16)
            rl = jnp.where(left, rtpair, 0.0).astype(BF16)
            rr = jnp.where(left, 0.0, rtpair).astype(BF16)
            mh, ech, wh = [], [], []
            for h in (h0, h0 + 1):
                bc = jnp.broadcast_to(acs[:, h:h + 1], (lc, lc))
                seg = jnp.where(lower, bc - acs_t[h:h + 1, :], -jnp.inf)
                mh.append((cbg * jnp.exp(seg) * dt_t[h:h + 1, :]).astype(BF16))
                ech.append((cg * jnp.exp(bc)).astype(BF16))
                wh.append((bg_t * w_t[h:h + 1, :]).astype(BF16))
            ypair = _dot(jnp.concatenate(mh + ech, axis=1),
                         jnp.concatenate([xl, xr, rl, rr], axis=0))
            spair = _dot(jnp.concatenate(wh, axis=1), jnp.concatenate([xl, xr], axis=0))
            e0 = jnp.broadcast_to(etot_t[h0:h0 + 1, :], (1, LANES))
            e1 = jnp.broadcast_to(etot_t[h0 + 1:h0 + 2, :], (1, LANES))
            rt_ref[:, x0:x0 + LANES] = rtpair * jnp.where(left, e0, e1) + spair
            yp = ypair + dexp_ref[:, x0:x0 + LANES] * xpair
            ybuf_ref[:, x0:x0 + LANES] = yp * _silu(z_ref[:, x0:x0 + LANES].astype(F32))

    for g in range(SSD_GROUPS):
        g0 = g * SSD_GROUP_W
        yg = ybuf_ref[:, g0:g0 + SSD_GROUP_W]
        y_ref[:, g0:g0 + SSD_GROUP_W] = _rms(
            yg, ng_ref[:, g0:g0 + SSD_GROUP_W], SUBLN_EPS).astype(y_ref.dtype)


def _ssd(proj, dt_raw, conv_w, conv_b, dt_bias, a_log, d_exp, norm_g, batch, seq):
    t = batch * seq
    lc = SSD_CHUNK
    nc = seq // lc
    row = lambda b, c: b * nc + c
    return pl.pallas_call(
        _ssd_body,
        grid=(batch, nc),
        in_specs=[
            pl.BlockSpec((lc, SSD_CONV_DIM), lambda b, c: (row(b, c), PROJ_XBC_OFF // SSD_CONV_DIM)),
            pl.BlockSpec((lc, SSD_INNER), lambda b, c: (row(b, c), PROJ_Z_OFF // SSD_INNER)),
            pl.BlockSpec((lc, LANES), lambda b, c: (row(b, c), 0)),
            pl.BlockSpec((SSD_CONV, SSD_CONV_DIM), lambda b, c: (0, 0)),
            pl.BlockSpec((1, SSD_CONV_DIM), lambda b, c: (0, 0)),
            pl.BlockSpec((1, LANES), lambda b, c: (0, 0)),
            pl.BlockSpec((1, LANES), lambda b, c: (0, 0)),
            pl.BlockSpec((1, SSD_INNER), lambda b, c: (0, 0)),
            pl.BlockSpec((1, SSD_INNER), lambda b, c: (0, 0)),
        ],
        out_specs=pl.BlockSpec((lc, SSD_INNER), lambda b, c: (row(b, c), 0)),
        out_shape=jax.ShapeDtypeStruct((t, SSD_INNER), BF16),
        scratch_shapes=[
            pltpu.VMEM((SUBLANES + lc, SSD_CONV_DIM), F32),
            pltpu.VMEM((SSD_STATE, SSD_INNER), F32),
            pltpu.VMEM((lc, SSD_INNER), F32),
        ],
        compiler_params=pltpu.CompilerParams(
            dimension_semantics=("parallel", "arbitrary"), vmem_limit_bytes=VMEM_LIMIT),
        name="ssd",
    )(proj, proj, dt_raw, conv_w, conv_b, dt_bias, a_log, d_exp, norm_g)


def _merge_body(ao_ref, ys_ref, gl_ref, bg_ref, x_ref, wa_ref, ws_ref, wo_ref, post_ref, o_ref):
    attn_out = _dot(ao_ref[...], wa_ref[...])
    ssd_out = _dot(ys_ref[...], ws_ref[...])
    gates = jax.nn.sigmoid(gl_ref[...].astype(F32) + bg_ref[...])
    mixed = gates[:, :D_MODEL] * attn_out + gates[:, D_MODEL:] * ssd_out
    mixed = _dot(mixed.astype(BF16), wo_ref[...])
    o_ref[...] = x_ref[...] + _rms(mixed, post_ref[...], NORM_EPS)


def _merge(attn_o, y_ssd, proj, b_gate, x, w_attn, w_ssd, w_out, post_g):
    t = x.shape[0]
    tm = MIX_TM
    gw = N_BRANCH * D_MODEL
    return pl.pallas_call(
        _merge_body,
        grid=(t // tm,),
        in_specs=[
            pl.BlockSpec((tm, DA_V), lambda i: (i, 0)),
            pl.BlockSpec((tm, SSD_INNER), lambda i: (i, 0)),
            pl.BlockSpec((tm, gw), lambda i: (i, PROJ_GATE_OFF // gw)),
            _const_spec((1, gw)),
            pl.BlockSpec((tm, D_MODEL), lambda i: (i, 0)),
            _const_spec((DA_V, D_MODEL)),
            _const_spec((SSD_INNER, D_MODEL)),
            _const_spec((D_MODEL, D_MODEL)),
            _const_spec((1, D_MODEL)),
        ],
        out_specs=pl.BlockSpec((tm, D_MODEL), lambda i: (i, 0)),
        out_shape=jax.ShapeDtypeStruct((t, D_MODEL), F32),
        compiler_params=pltpu.CompilerParams(
            dimension_semantics=("parallel",), vmem_limit_bytes=VMEM_LIMIT),
        name="merge",
    )(attn_o, y_ssd, proj, b_gate, x, w_attn, w_ssd, w_out, post_g)


def _xattn_body(x_ref, pre_ref, wq_ref, kv_ref, wo_ref, post_ref, o_ref):
    x = x_ref[...]
    hq = _rms(x, pre_ref[...], NORM_EPS).astype(BF16)
    qx = (_dot(hq, wq_ref[...]) * (XA_HEAD_DIM ** -0.5)).astype(BF16)
    heads = []
    for h in range(XA_HEADS):
        c0 = h * XA_HEAD_DIM
        s = _dot_nt(qx[:, c0:c0 + XA_HEAD_DIM], kv_ref[:, c0:c0 + XA_HEAD_DIM])
        e = jnp.exp(s - jnp.max(s, axis=-1, keepdims=True))
        denom = jnp.sum(e, axis=-1, keepdims=True)
        oh = _dot(e.astype(BF16), kv_ref[:, D_MODEL + c0:D_MODEL + c0 + XA_HEAD_DIM])
        heads.append((oh / denom).astype(BF16))
    xo = _dot(jnp.concatenate(heads, axis=1), wo_ref[...])
    o_ref[...] = x + _rms(xo, post_ref[...], NORM_EPS)


def _xattn(x, pre_g, w_q, kv, w_o, post_g, seq):
    t = x.shape[0]
    tm = MIX_TM
    per_batch = seq // tm
    return pl.pallas_call(
        _xattn_body,
        grid=(t // tm,),
        in_specs=[
            pl.BlockSpec((tm, D_MODEL), lambda i: (i, 0)),
            _const_spec((1, D_MODEL)),
            _const_spec((D_MODEL, D_MODEL)),
            pl.BlockSpec((MEM_LEN, 2 * D_MODEL), lambda i: (i // per_batch, 0)),
            _const_spec((D_MODEL, D_MODEL)),
            _const_spec((1, D_MODEL)),
        ],
        out_specs=pl.BlockSpec((tm, D_MODEL), lambda i: (i, 0)),
        out_shape=jax.ShapeDtypeStruct((t, D_MODEL), F32),
        compiler_params=pltpu.CompilerParams(
            dimension_semantics=("parallel",), vmem_limit_bytes=VMEM_LIMIT),
        name="cross_attn",
    )(x, pre_g, w_q, kv, w_o, post_g)


def _row(v):
    return v.reshape(1, -1)


def _pad_lanes(v):
    return jnp.pad(v, ((0, 0), (0, LANES - v.shape[1])))


def _layer(x, mem, layer_idx, batch, seq,
           ffn1_pre_g, ffn1_post_g, ffn1_w_gu, ffn1_w_down,
           mix_pre_g, mix_post_g, w_in, b_gate,
           da_lambda_q1, da_lambda_k1, da_lambda_q2, da_lambda_k2, da_subln_g,
           ssd_conv_w, ssd_conv_b, ssd_dt_bias, ssd_A_log, ssd_D, ssd_norm_g,
           w_branch_attn, w_branch_ssd, w_mix_out,
           xa_pre_g, xa_post_g, mem_norm_g, xa_w_q, xa_w_kv, xa_w_o,
           ffn2_pre_g, ffn2_post_g, ffn2_w_gu, ffn2_w_down):
    x = _ffn(x, _row(ffn1_pre_g), ffn1_w_gu.astype(BF16), ffn1_w_down.astype(BF16),
             _row(ffn1_post_g))

    z0 = 3 * DA_QK
    xbc0 = z0 + SSD_INNER
    dt0 = xbc0 + SSD_CONV_DIM
    gate0 = dt0 + SSD_HEADS
    w_main = jnp.concatenate(
        [w_in[:, :z0], w_in[:, xbc0:dt0], w_in[:, z0:xbc0], w_in[:, gate0:]], axis=1).astype(BF16)
    w_dt = _pad_lanes(w_in[:, dt0:gate0]).astype(BF16)
    proj, dt_raw = _norm_proj(x, _row(mix_pre_g), w_main, w_dt, PROJ_TM, PROJ_TN)

    lam_init = 0.8 - 0.6 * math.exp(-0.3 * layer_idx)
    lam_params = jnp.stack([da_lambda_q1, da_lambda_k1, da_lambda_q2, da_lambda_k2])
    attn_o = _diff_attn(proj, lam_params, _row(da_subln_g), lam_init, batch, seq)

    y_ssd = _ssd(proj, dt_raw, ssd_conv_w, _row(ssd_conv_b), _pad_lanes(_row(ssd_dt_bias)),
                 _pad_lanes(_row(ssd_A_log)), _row(jnp.repeat(ssd_D, SSD_HEAD_DIM)),
                 _row(ssd_norm_g), batch, seq)

    x = _merge(attn_o, y_ssd, proj, _row(b_gate), x, w_branch_attn.astype(BF16),
               w_branch_ssd.astype(BF16), w_mix_out.astype(BF16), _row(mix_post_g))

    mem2 = mem.reshape(batch * MEM_LEN, D_MODEL)
    kv, _ = _norm_proj(mem2, _row(mem_norm_g), xa_w_kv.astype(BF16),
                       jnp.zeros((D_MODEL, LANES), BF16), MEM_LEN, 2 * D_MODEL)
    x = _xattn(x, _row(xa_pre_g), xa_w_q.astype(BF16), kv, xa_w_o.astype(BF16),
               _row(xa_post_g), seq)

    x = _ffn(x, _row(ffn2_pre_g), ffn2_w_gu.astype(BF16), ffn2_w_down.astype(BF16),
             _row(ffn2_post_g))
    return x


def kernel(x, mem, ffn1_pre_g, ffn1_post_g, ffn1_w_gu, ffn1_w_down, mix_pre_g, mix_post_g, w_in, b_gate, da_lambda_q1, da_lambda_k1, da_lambda_q2, da_lambda_k2, da_subln_g, ssd_conv_w, ssd_conv_b, ssd_dt_bias, ssd_A_log, ssd_D, ssd_norm_g, w_branch_attn, w_branch_ssd, w_mix_out, xa_pre_g, xa_post_g, mem_norm_g, xa_w_q, xa_w_kv, xa_w_o, ffn2_pre_g, ffn2_post_g, ffn2_w_gu, ffn2_w_down):
    batch, seq, d = x.shape
    params = (ffn1_pre_g, ffn1_post_g, ffn1_w_gu, ffn1_w_down, mix_pre_g, mix_post_g, w_in, b_gate,
              da_lambda_q1, da_lambda_k1, da_lambda_q2, da_lambda_k2, da_subln_g,
              ssd_conv_w, ssd_conv_b, ssd_dt_bias, ssd_A_log, ssd_D, ssd_norm_g,
              w_branch_attn, w_branch_ssd, w_mix_out,
              xa_pre_g, xa_post_g, mem_norm_g, xa_w_q, xa_w_kv, xa_w_o,
              ffn2_pre_g, ffn2_post_g, ffn2_w_gu, ffn2_w_down)
    h = x.reshape(batch * seq, d)
    for layer in range(ffn1_pre_g.shape[0]):
        h = _layer(h, mem, layer, batch, seq, *[p[layer] for p in params])
    return h.reshape(batch, seq, d)
```

```python
import functools
import math

import jax
import jax.numpy as jnp
from jax import lax
from jax.experimental import pallas as pl
from jax.experimental.pallas import tpu as pltpu

F32 = jnp.float32
BF16 = jnp.bfloat16

D_MODEL = 1024
MEM_LEN = 256
DA_HEADS = 8
DA_HEAD_DIM = 64
DA_PAIR = 2 * DA_HEAD_DIM
DA_QK = DA_HEADS * DA_PAIR
DA_V = DA_HEADS * DA_PAIR
SSD_INNER = 2 * D_MODEL
SSD_HEAD_DIM = 64
SSD_HEADS = SSD_INNER // SSD_HEAD_DIM
SSD_GROUPS = 4
SSD_HEADS_PER_GROUP = SSD_HEADS // SSD_GROUPS
SSD_STATE = 128
SSD_CONV = 4
SSD_CHUNK = 128
SSD_GROUP_W = SSD_INNER // SSD_GROUPS
SSD_HIST = 16
SSD_CONV_DIM = SSD_INNER + 2 * SSD_GROUPS * SSD_STATE
XA_HEADS = 4
XA_HEAD_DIM = D_MODEL // XA_HEADS
D_FF = 2816
N_BRANCH = 2
NORM_EPS = 1e-6
SUBLN_EPS = 1e-5

LANES = 128
SUBLANES = 8
VMEM_LIMIT = 56 * 1024 * 1024

PROJ_XBC_OFF = 3 * DA_QK
PROJ_Z_OFF = PROJ_XBC_OFF + SSD_CONV_DIM
PROJ_GATE_OFF = PROJ_Z_OFF + SSD_INNER
PROJ_WIDTH = PROJ_GATE_OFF + N_BRANCH * D_MODEL

FFN_TM = 512
FFN_CHUNK = 256
PROJ_TM = 1024
PROJ_TN = 2048
ATTN_TQ = 512
ATTN_KSLAB = 256
MIX_TM = 512


def _rms(x, g, eps):
    return x * lax.rsqrt(jnp.mean(x * x, axis=-1, keepdims=True) + eps) * g


def _silu(x):
    h = 0.5 * x
    return h + h * jnp.tanh(h)


def _dot(a, b):
    return jnp.dot(a, b, preferred_element_type=F32)


def _dot_nt(a, b):
    return lax.dot_general(a, b, (((1,), (1,)), ((), ())), preferred_element_type=F32)


def _const_spec(shape):
    nd = len(shape)
    return pl.BlockSpec(shape, lambda *_: (0,) * nd, pipeline_mode=pl.Buffered(1))


def _ffn_body(x_ref, pre_ref, wgu_ref, wd_ref, post_ref, o_ref):
    x = x_ref[...]
    hn = _rms(x, pre_ref[...], NORM_EPS).astype(BF16)
    acc = None
    for c0 in range(0, D_FF, FFN_CHUNK):
        g = _dot(hn, wgu_ref[:, c0:c0 + FFN_CHUNK])
        u = _dot(hn, wgu_ref[:, D_FF + c0:D_FF + c0 + FFN_CHUNK])
        a = (_silu(g) * u).astype(BF16)
        d = _dot(a, wd_ref[c0:c0 + FFN_CHUNK, :])
        acc = d if acc is None else acc + d
    o_ref[...] = x + 0.5 * _rms(acc, post_ref[...], NORM_EPS)


def _ffn(x, pre_g, w_gu, w_down, post_g):
    t = x.shape[0]
    return pl.pallas_call(
        _ffn_body,
        grid=(t // FFN_TM,),
        in_specs=[
            pl.BlockSpec((FFN_TM, D_MODEL), lambda i: (i, 0)),
            _const_spec((1, D_MODEL)),
            _const_spec((D_MODEL, 2 * D_FF)),
            _const_spec((D_FF, D_MODEL)),
            _const_spec((1, D_MODEL)),
        ],
        out_specs=pl.BlockSpec((FFN_TM, D_MODEL), lambda i: (i, 0)),
        out_shape=jax.ShapeDtypeStruct((t, D_MODEL), F32),
        compiler_params=pltpu.CompilerParams(
            dimension_semantics=("parallel",), vmem_limit_bytes=VMEM_LIMIT),
        name="ffn",
    )(x, pre_g, w_gu, w_down, post_g)


def _proj_body(x_ref, g_ref, w_ref, ws_ref, o_ref, os_ref, hn_ref):
    @pl.when(pl.program_id(1) == 0)
    def _():
        hn = _rms(x_ref[...], g_ref[...], NORM_EPS).astype(BF16)
        hn_ref[...] = hn
        os_ref[...] = _dot(hn, ws_ref[...])

    o_ref[...] = _dot(hn_ref[...], w_ref[...]).astype(o_ref.dtype)


def _norm_proj(x, g, w, w_side, tm, tn):
    t, n = x.shape[0], w.shape[1]
    ns = w_side.shape[1]
    return pl.pallas_call(
        _proj_body,
        grid=(t // tm, n // tn),
        in_specs=[
            pl.BlockSpec((tm, D_MODEL), lambda i, j: (i, 0)),
            pl.BlockSpec((1, D_MODEL), lambda i, j: (0, 0)),
            pl.BlockSpec((D_MODEL, tn), lambda i, j: (0, j)),
            pl.BlockSpec((D_MODEL, ns), lambda i, j: (0, 0)),
        ],
        out_specs=[
            pl.BlockSpec((tm, tn), lambda i, j: (i, j)),
            pl.BlockSpec((tm, ns), lambda i, j: (i, 0)),
        ],
        out_shape=[
            jax.ShapeDtypeStruct((t, n), BF16),
            jax.ShapeDtypeStruct((t, ns), F32),
        ],
        scratch_shapes=[pltpu.VMEM((tm, D_MODEL), BF16)],
        compiler_params=pltpu.CompilerParams(
            dimension_semantics=("parallel", "arbitrary"), vmem_limit_bytes=VMEM_LIMIT),
        name="norm_proj",
    )(x, g, w, w_side)


def _attn_body(lam_init, seq, q_ref, k_ref, v_ref, lam_ref, g_ref, o_ref,
               qq_ref, m_ref, acc_ref, vx_ref):
    tq = ATTN_TQ
    lamv = lam_ref[...]
    lam = (jnp.exp(jnp.sum(lamv[0:1] * lamv[1:2], axis=-1, keepdims=True))
           - jnp.exp(jnp.sum(lamv[2:3] * lamv[3:4], axis=-1, keepdims=True)) + lam_init)
    lane = lax.broadcasted_iota(jnp.int32, (1, DA_PAIR), 1)
    first_half = lane < DA_HEAD_DIM
    vx_ref[:, :DA_PAIR] = v_ref[...]
    vx_ref[:, DA_PAIR:] = jnp.broadcast_to(
        jnp.where(lane == 0, 1.0, 0.0), (seq, DA_PAIR)).astype(BF16)

    def kv_block(k0, width, masked):
        s = jnp.concatenate(
            [_dot_nt(qq_ref[...], k_ref[pl.ds(pl.multiple_of(k0 + c, ATTN_KSLAB), ATTN_KSLAB), :])
             for c in range(0, width, ATTN_KSLAB)], axis=1)
        if masked:
            row = lax.broadcasted_iota(jnp.int32, (2 * tq, width), 0)
            col = lax.broadcasted_iota(jnp.int32, (2 * tq, width), 1)
            s = jnp.where(jnp.where(row >= tq, row - tq, row) >= col, s, -jnp.inf)
        m_old = m_ref[...]
        m_new = jnp.maximum(m_old, jnp.max(s, axis=-1, keepdims=True))
        alpha = jnp.exp2(m_old - m_new)
        p = jnp.concatenate(
            [jnp.exp2(s[:, c:c + LANES] - m_new).astype(BF16) for c in range(0, width, LANES)],
            axis=1)
        vb = vx_ref[pl.ds(k0, width), :]
        alpha2 = jnp.concatenate([alpha, alpha], axis=1)
        for r in (0, tq):
            acc_ref[r:r + tq, :] = acc_ref[r:r + tq, :] * alpha2[r:r + tq] + _dot(p[r:r + tq], vb)
        m_ref[...] = m_new

    def q_tile(i, carry):
        r0 = pl.multiple_of(i * tq, tq)
        q = q_ref[pl.ds(r0, tq), :]
        qq_ref[0:tq, :] = jnp.where(first_half, q, jnp.zeros_like(q))
        qq_ref[tq:2 * tq, :] = jnp.where(first_half, jnp.zeros_like(q), q)
        m_ref[...] = jnp.full(m_ref.shape, -jnp.inf, F32)
        acc_ref[...] = jnp.zeros(acc_ref.shape, F32)

        def double_step(jj, c):
            kv_block(pl.multiple_of(jj * (2 * tq), 2 * tq), 2 * tq, False)
            return c

        lax.fori_loop(0, lax.shift_right_logical(i, 1), double_step, 0)

        @pl.when((i & 1) == 1)
        def _():
            kv_block(pl.multiple_of((i - 1) * tq, tq), tq, False)

        kv_block(r0, tq, True)
        acc = acc_ref[...]
        o = acc[:, :DA_PAIR] / acc[:, DA_PAIR:DA_PAIR + 1]
        res = o[:tq] - lam * o[tq:]
        out = _rms(res, g_ref[...], SUBLN_EPS) * (1.0 - lam_init)
        o_ref[pl.ds(r0, tq), :] = out.astype(o_ref.dtype)
        return carry

    lax.fori_loop(0, seq // tq, q_tile, 0)


def _diff_attn(proj, lam_params, subln_g, lam_init, batch, seq):
    t = batch * seq
    tq = ATTN_TQ
    return pl.pallas_call(
        functools.partial(_attn_body, lam_init, seq),
        grid=(batch, DA_HEADS),
        in_specs=[
            pl.BlockSpec((seq, DA_PAIR), lambda b, h: (b, h)),
            pl.BlockSpec((seq, DA_PAIR), lambda b, h: (b, DA_HEADS + h)),
            pl.BlockSpec((seq, DA_PAIR), lambda b, h: (b, 2 * DA_HEADS + h)),
            pl.BlockSpec((4, DA_HEAD_DIM), lambda b, h: (0, 0)),
            pl.BlockSpec((1, DA_PAIR), lambda b, h: (0, 0)),
        ],
        out_specs=pl.BlockSpec((seq, DA_PAIR), lambda b, h: (b, h)),
        out_shape=jax.ShapeDtypeStruct((t, DA_V), BF16),
        scratch_shapes=[
            pltpu.VMEM((2 * tq, DA_PAIR), BF16),
            pltpu.VMEM((2 * tq, LANES), F32),
            pltpu.VMEM((2 * tq, 2 * DA_PAIR), F32),
            pltpu.VMEM((seq, 2 * DA_PAIR), BF16),
        ],
        compiler_params=pltpu.CompilerParams(
            dimension_semantics=("parallel", "parallel"), vmem_limit_bytes=VMEM_LIMIT),
        name="diff_attn",
    )(proj, proj, proj, lam_params, subln_g)


def _ssd_body(xbc_ref, z_ref, dt_ref, shift_ref, cw_ref, cb_ref, dtb_ref, alog_ref, dexp_ref,
              ng_ref, y_ref, xe_ref, rt_ref, ybuf_ref):
    lc = SSD_CHUNK
    hist = SSD_HIST

    @pl.when(pl.program_id(1) == 0)
    def _():
        xe_ref[0:hist, :] = jnp.zeros((hist, SSD_CONV_DIM), BF16)
        rt_ref[...] = jnp.zeros(rt_ref.shape, F32)

    xb = xbc_ref[...]
    xe_ref[hist:hist + lc, :] = xb
    shifted = _dot(shift_ref[...], xe_ref[...])
    xe_ref[0:hist, :] = xb[lc - hist:lc, :]
    cw = cw_ref[...]
    conv = cb_ref[...] + cw[SSD_CONV - 1:SSD_CONV, :] * xb.astype(F32)
    for k in range(SSD_CONV - 1):
        conv = conv + cw[k:k + 1, :] * shifted[k * lc:(k + 1) * lc, :]
    act = _silu(conv)

    dt = jax.nn.softplus(dt_ref[...] + dtb_ref[...])
    adt = -jnp.exp(alog_ref[...]) * dt
    ri = lax.broadcasted_iota(jnp.int32, (lc, lc), 0)
    ci = lax.broadcasted_iota(jnp.int32, (lc, lc), 1)
    lower = ri >= ci
    tril = jnp.where(lower, 1.0, 0.0)
    acs = jnp.dot(tril, adt, precision=lax.Precision.HIGHEST, preferred_element_type=F32)
    acs_t = acs.T
    dt_t = dt.T
    tot_t = acs_t[:, lc - 1:lc]
    w_t = jnp.exp(tot_t - acs_t) * dt_t
    etot_t = jnp.exp(tot_t)
    src_t = acs_t - jnp.log(dt_t)

    lane = lax.broadcasted_iota(jnp.int32, (lc, LANES), 1)
    left = lane < SSD_HEAD_DIM
    zero16 = jnp.zeros((lc, LANES), BF16)

    for g in range(SSD_GROUPS):
        b0 = SSD_INNER + g * SSD_STATE
        c0 = SSD_INNER + SSD_GROUPS * SSD_STATE + g * SSD_STATE
        cg = act[:, c0:c0 + SSD_STATE]
        bg_t = act[:, b0:b0 + SSD_STATE].T
        cbg = _dot(cg.astype(BF16), bg_t.astype(BF16))
        for pr in range(SSD_HEADS_PER_GROUP // 2):
            h0 = g * SSD_HEADS_PER_GROUP + 2 * pr
            x0 = h0 * SSD_HEAD_DIM
            xpair = act[:, x0:x0 + LANES]
            rtpair = rt_ref[:, x0:x0 + LANES]
            x16 = xpair.astype(BF16)
            r16 = rtpair.astype(BF16)
            xl, xr = jnp.where(left, x16, zero16), jnp.where(left, zero16, x16)
            rl, rr = jnp.where(left, r16, zero16), jnp.where(left, zero16, r16)
            mh, ech, wh = [], [], []
            for h in (h0, h0 + 1):
                bc = jnp.broadcast_to(acs[:, h:h + 1], (lc, lc))
                seg = jnp.where(lower, bc - src_t[h:h + 1, :], -jnp.inf)
                mh.append((cbg * jnp.exp(seg)).astype(BF16))
                ech.append((cg * jnp.exp(bc)).astype(BF16))
                wh.append((bg_t * w_t[h:h + 1, :]).astype(BF16))
            ypair = _dot(jnp.concatenate(mh + ech, axis=1),
                         jnp.concatenate([xl, xr, rl, rr], axis=0))
            spair = _dot(jnp.concatenate(wh, axis=1), jnp.concatenate([xl, xr], axis=0))
            e0 = jnp.broadcast_to(etot_t[h0:h0 + 1, :], (1, LANES))
            e1 = jnp.broadcast_to(etot_t[h0 + 1:h0 + 2, :], (1, LANES))
            rt_ref[:, x0:x0 + LANES] = rtpair * jnp.where(left[0:1], e0, e1) + spair
            yp = ypair + dexp_ref[:, x0:x0 + LANES] * xpair
            ybuf_ref[:, x0:x0 + LANES] = yp * _silu(z_ref[:, x0:x0 + LANES].astype(F32))

    for g in range(SSD_GROUPS):
        g0 = g * SSD_GROUP_W
        yg = ybuf_ref[:, g0:g0 + SSD_GROUP_W]
        y_ref[:, g0:g0 + SSD_GROUP_W] = _rms(
            yg, ng_ref[:, g0:g0 + SSD_GROUP_W], SUBLN_EPS).astype(y_ref.dtype)


def _ssd(proj, dt_raw, conv_w, conv_b, dt_bias, a_log, d_exp, norm_g, batch, seq):
    t = batch * seq
    lc = SSD_CHUNK
    nc = seq // lc
    row = lambda b, c: b * nc + c
    t_idx = jnp.arange((SSD_CONV - 1) * lc)
    src = SSD_HIST + t_idx % lc - (SSD_CONV - 1) + t_idx // lc
    shift = (src[:, None] == jnp.arange(SSD_HIST + lc)[None, :]).astype(BF16)
    return pl.pallas_call(
        _ssd_body,
        grid=(batch, nc),
        in_specs=[
            pl.BlockSpec((lc, SSD_CONV_DIM), lambda b, c: (row(b, c), PROJ_XBC_OFF // SSD_CONV_DIM)),
            pl.BlockSpec((lc, SSD_INNER), lambda b, c: (row(b, c), PROJ_Z_OFF // SSD_INNER)),
            pl.BlockSpec((lc, LANES), lambda b, c: (row(b, c), 0)),
            pl.BlockSpec(((SSD_CONV - 1) * lc, SSD_HIST + lc), lambda b, c: (0, 0)),
            pl.BlockSpec((SSD_CONV, SSD_CONV_DIM), lambda b, c: (0, 0)),
            pl.BlockSpec((1, SSD_CONV_DIM), lambda b, c: (0, 0)),
            pl.BlockSpec((1, LANES), lambda b, c: (0, 0)),
            pl.BlockSpec((1, LANES), lambda b, c: (0, 0)),
            pl.BlockSpec((1, SSD_INNER), lambda b, c: (0, 0)),
            pl.BlockSpec((1, SSD_INNER), lambda b, c: (0, 0)),
        ],
        out_specs=pl.BlockSpec((lc, SSD_INNER), lambda b, c: (row(b, c), 0)),
        out_shape=jax.ShapeDtypeStruct((t, SSD_INNER), BF16),
        scratch_shapes=[
            pltpu.VMEM((SSD_HIST + lc, SSD_CONV_DIM), BF16),
            pltpu.VMEM((SSD_STATE, SSD_INNER), F32),
            pltpu.VMEM((lc, SSD_INNER), F32),
        ],
        compiler_params=pltpu.CompilerParams(
            dimension_semantics=("parallel", "arbitrary"), vmem_limit_bytes=VMEM_LIMIT),
        name="ssd",
    )(proj, proj, dt_raw, shift, conv_w, conv_b, dt_bias, a_log, d_exp, norm_g)


def _merge_body(ao_ref, ys_ref, gl_ref, bg_ref, x_ref, wa_ref, ws_ref, wo_ref, post_ref, o_ref):
    attn_out = _dot(ao_ref[...], wa_ref[...])
    ssd_out = _dot(ys_ref[...], ws_ref[...])
    gates = jax.nn.sigmoid(gl_ref[...].astype(F32) + bg_ref[...])
    mixed = gates[:, :D_MODEL] * attn_out + gates[:, D_MODEL:] * ssd_out
    mixed = _dot(mixed.astype(BF16), wo_ref[...])
    o_ref[...] = x_ref[...] + _rms(mixed, post_ref[...], NORM_EPS)


def _merge(attn_o, y_ssd, proj, b_gate, x, w_attn, w_ssd, w_out, post_g):
    t = x.shape[0]
    tm = MIX_TM
    gw = N_BRANCH * D_MODEL
    return pl.pallas_call(
        _merge_body,
        grid=(t // tm,),
        in_specs=[
            pl.BlockSpec((tm, DA_V), lambda i: (i, 0)),
            pl.BlockSpec((tm, SSD_INNER), lambda i: (i, 0)),
            pl.BlockSpec((tm, gw), lambda i: (i, PROJ_GATE_OFF // gw)),
            _const_spec((1, gw)),
            pl.BlockSpec((tm, D_MODEL), lambda i: (i, 0)),
            _const_spec((DA_V, D_MODEL)),
            _const_spec((SSD_INNER, D_MODEL)),
            _const_spec((D_MODEL, D_MODEL)),
            _const_spec((1, D_MODEL)),
        ],
        out_specs=pl.BlockSpec((tm, D_MODEL), lambda i: (i, 0)),
        out_shape=jax.ShapeDtypeStruct((t, D_MODEL), F32),
        compiler_params=pltpu.CompilerParams(
            dimension_semantics=("parallel",), vmem_limit_bytes=VMEM_LIMIT),
        name="merge",
    )(attn_o, y_ssd, proj, b_gate, x, w_attn, w_ssd, w_out, post_g)


def _xattn_body(x_ref, pre_ref, wq_ref, kv_ref, wo_ref, post_ref, o_ref):
    x = x_ref[...]
    hq = _rms(x, pre_ref[...], NORM_EPS).astype(BF16)
    qx = (_dot(hq, wq_ref[...]) * (XA_HEAD_DIM ** -0.5)).astype(BF16)
    heads = []
    for h in range(XA_HEADS):
        c0 = h * XA_HEAD_DIM
        s = _dot_nt(qx[:, c0:c0 + XA_HEAD_DIM], kv_ref[:, c0:c0 + XA_HEAD_DIM])
        e = jnp.exp(s - jnp.max(s, axis=-1, keepdims=True))
        denom = jnp.sum(e, axis=-1, keepdims=True)
        oh = _dot(e.astype(BF16), kv_ref[:, D_MODEL + c0:D_MODEL + c0 + XA_HEAD_DIM])
        heads.append((oh / denom).astype(BF16))
    xo = _dot(jnp.concatenate(heads, axis=1), wo_ref[...])
    o_ref[...] = x + _rms(xo, post_ref[...], NORM_EPS)


def _xattn(x, pre_g, w_q, kv, w_o, post_g, seq):
    t = x.shape[0]
    tm = MIX_TM
    per_batch = seq // tm
    return pl.pallas_call(
        _xattn_body,
        grid=(t // tm,),
        in_specs=[
            pl.BlockSpec((tm, D_MODEL), lambda i: (i, 0)),
            _const_spec((1, D_MODEL)),
            _const_spec((D_MODEL, D_MODEL)),
            pl.BlockSpec((MEM_LEN, 2 * D_MODEL), lambda i: (i // per_batch, 0)),
            _const_spec((D_MODEL, D_MODEL)),
            _const_spec((1, D_MODEL)),
        ],
        out_specs=pl.BlockSpec((tm, D_MODEL), lambda i: (i, 0)),
        out_shape=jax.ShapeDtypeStruct((t, D_MODEL), F32),
        compiler_params=pltpu.CompilerParams(
            dimension_semantics=("parallel",), vmem_limit_bytes=VMEM_LIMIT),
        name="cross_attn",
    )(x, pre_g, w_q, kv, w_o, post_g)


def _row(v):
    return v.reshape(1, -1)


def _pad_lanes(v):
    return jnp.pad(v, ((0, 0), (0, LANES - v.shape[1])))


def _layer(x, mem, layer_idx, batch, seq,
           ffn1_pre_g, ffn1_post_g, ffn1_w_gu, ffn1_w_down,
           mix_pre_g, mix_post_g, w_in, b_gate,
           da_lambda_q1, da_lambda_k1, da_lambda_q2, da_lambda_k2, da_subln_g,
           ssd_conv_w, ssd_conv_b, ssd_dt_bias, ssd_A_log, ssd_D, ssd_norm_g,
           w_branch_attn, w_branch_ssd, w_mix_out,
           xa_pre_g, xa_post_g, mem_norm_g, xa_w_q, xa_w_kv, xa_w_o,
           ffn2_pre_g, ffn2_post_g, ffn2_w_gu, ffn2_w_down):
    x = _ffn(x, _row(ffn1_pre_g), ffn1_w_gu.astype(BF16), ffn1_w_down.astype(BF16),
             _row(ffn1_post_g))

    z0 = 3 * DA_QK
    xbc0 = z0 + SSD_INNER
    dt0 = xbc0 + SSD_CONV_DIM
    gate0 = dt0 + SSD_HEADS
    q_scale = DA_HEAD_DIM ** -0.5 * math.log2(math.e)
    w_main = jnp.concatenate(
        [w_in[:, :DA_QK] * q_scale, w_in[:, DA_QK:z0], w_in[:, xbc0:dt0], w_in[:, z0:xbc0],
         w_in[:, gate0:]], axis=1).astype(BF16)
    w_dt = _pad_lanes(w_in[:, dt0:gate0]).astype(BF16)
    proj, dt_raw = _norm_proj(x, _row(mix_pre_g), w_main, w_dt, PROJ_TM, PROJ_TN)

    lam_init = 0.8 - 0.6 * math.exp(-0.3 * layer_idx)
    lam_params = jnp.stack([da_lambda_q1, da_lambda_k1, da_lambda_q2, da_lambda_k2])
    attn_o = _diff_attn(proj, lam_params, _row(da_subln_g), lam_init, batch, seq)

    y_ssd = _ssd(proj, dt_raw, ssd_conv_w, _row(ssd_conv_b), _pad_lanes(_row(ssd_dt_bias)),
                 _pad_lanes(_row(ssd_A_log)), _row(jnp.repeat(ssd_D, SSD_HEAD_DIM)),
                 _row(ssd_norm_g), batch, seq)

    x = _merge(attn_o, y_ssd, proj, _row(b_gate), x, w_branch_attn.astype(BF16),
               w_branch_ssd.astype(BF16), w_mix_out.astype(BF16), _row(mix_post_g))

    mem2 = mem.reshape(batch * MEM_LEN, D_MODEL)
    kv, _ = _norm_proj(mem2, _row(mem_norm_g), xa_w_kv.astype(BF16),
                       jnp.zeros((D_MODEL, LANES), BF16), MEM_LEN, 2 * D_MODEL)
    x = _xattn(x, _row(xa_pre_g), xa_w_q.astype(BF16), kv, xa_w_o.astype(BF16),
               _row(xa_post_g), seq)

    x = _ffn(x, _row(ffn2_pre_g), ffn2_w_gu.astype(BF16), ffn2_w_down.astype(BF16),
             _row(ffn2_post_g))
    return x


def kernel(x, mem, ffn1_pre_g, ffn1_post_g, ffn1_w_gu, ffn1_w_down, mix_pre_g, mix_post_g, w_in, b_gate, da_lambda_q1, da_lambda_k1, da_lambda_q2, da_lambda_k2, da_subln_g, ssd_conv_w, ssd_conv_b, ssd_dt_bias, ssd_A_log, ssd_D, ssd_norm_g, w_branch_attn, w_branch_ssd, w_mix_out, xa_pre_g, xa_post_g, mem_norm_g, xa_w_q, xa_w_kv, xa_w_o, ffn2_pre_g, ffn2_post_g, ffn2_w_gu, ffn2_w_down):
    batch, seq, d = x.shape
    params = (ffn1_pre_g, ffn1_post_g, ffn1_w_gu, ffn1_w_down, mix_pre_g, mix_post_g, w_in, b_gate,
              da_lambda_q1, da_lambda_k1, da_lambda_q2, da_lambda_k2, da_subln_g,
              ssd_conv_w, ssd_conv_b, ssd_dt_bias, ssd_A_log, ssd_D, ssd_norm_g,
              w_branch_attn, w_branch_ssd, w_mix_out,
              xa_pre_g, xa_post_g, mem_norm_g, xa_w_q, xa_w_kv, xa_w_o,
              ffn2_pre_g, ffn2_post_g, ffn2_w_gu, ffn2_w_down)
    h = x.reshape(batch * seq, d)
    for layer in range(ffn1_pre_g.shape[0]):
        h = _layer(h, mem, layer, batch, seq, *[p[layer] for p in params])
    return h.reshape(batch, seq, d)
```

```python
import functools
import math

import jax
import jax.numpy as jnp
from jax import lax
from jax.experimental import pallas as pl
from jax.experimental.pallas import tpu as pltpu

F32 = jnp.float32
BF16 = jnp.bfloat16

D_MODEL = 1024
MEM_LEN = 256
DA_HEADS = 8
DA_HEAD_DIM = 64
DA_PAIR = 2 * DA_HEAD_DIM
DA_QK = DA_HEADS * DA_PAIR
DA_V = DA_HEADS * DA_PAIR
SSD_INNER = 2 * D_MODEL
SSD_HEAD_DIM = 64
SSD_HEADS = SSD_INNER // SSD_HEAD_DIM
SSD_GROUPS = 4
SSD_HEADS_PER_GROUP = SSD_HEADS // SSD_GROUPS
SSD_STATE = 128
SSD_CONV = 4
SSD_CHUNK = 128
SSD_GROUP_W = SSD_INNER // SSD_GROUPS
SSD_HIST = 16
SSD_CONV_DIM = SSD_INNER + 2 * SSD_GROUPS * SSD_STATE
XA_HEADS = 4
XA_HEAD_DIM = D_MODEL // XA_HEADS
D_FF = 2816
N_BRANCH = 2
NORM_EPS = 1e-6
SUBLN_EPS = 1e-5

LANES = 128
SUBLANES = 8
VMEM_LIMIT = 56 * 1024 * 1024

PROJ_Z_OFF = 3 * DA_QK
PROJ_XBC_OFF = PROJ_Z_OFF + SSD_INNER
PROJ_GATE_OFF = PROJ_XBC_OFF + SSD_CONV_DIM
PROJ_WIDTH = PROJ_GATE_OFF + N_BRANCH * D_MODEL
PROJ_PIECE = 1024

FFN_TM = 1024
FFN_CHUNK = 256
PROJ_TM = 1024
PROJ_TN = 2048
PROJ_MAIN_TILES = PROJ_GATE_OFF // PROJ_TN
ATTN_TQ = 512
ATTN_KSLAB = 256
ATTN_HPS = 2
MIX_TM = 512


def _rms(x, g, eps):
    return x * lax.rsqrt(jnp.mean(x * x, axis=-1, keepdims=True) + eps) * g


def _silu(x):
    h = 0.5 * x
    return h + h * jnp.tanh(h)


def _dot(a, b):
    return jnp.dot(a, b, preferred_element_type=F32)


def _dot_nt(a, b):
    return lax.dot_general(a, b, (((1,), (1,)), ((), ())), preferred_element_type=F32)


def _const_spec(shape):
    nd = len(shape)
    return pl.BlockSpec(shape, lambda *_: (0,) * nd, pipeline_mode=pl.Buffered(1))


def _ffn_body(x_ref, pre_ref, wgu_ref, wd_ref, post_ref, o_ref):
    x = x_ref[...]
    hn = _rms(x, pre_ref[...], NORM_EPS).astype(BF16)
    acc = None
    for c0 in range(0, D_FF, FFN_CHUNK):
        g = _dot(hn, wgu_ref[:, c0:c0 + FFN_CHUNK])
        u = _dot(hn, wgu_ref[:, D_FF + c0:D_FF + c0 + FFN_CHUNK])
        a = (_silu(g) * u).astype(BF16)
        d = _dot(a, wd_ref[c0:c0 + FFN_CHUNK, :])
        acc = d if acc is None else acc + d
    o_ref[...] = x + 0.5 * _rms(acc, post_ref[...], NORM_EPS)


def _ffn(x, pre_g, w_gu, w_down, post_g):
    t = x.shape[0]
    return pl.pallas_call(
        _ffn_body,
        grid=(t // FFN_TM,),
        in_specs=[
            pl.BlockSpec((FFN_TM, D_MODEL), lambda i: (i, 0)),
            _const_spec((1, D_MODEL)),
            _const_spec((D_MODEL, 2 * D_FF)),
            _const_spec((D_FF, D_MODEL)),
            _const_spec((1, D_MODEL)),
        ],
        out_specs=pl.BlockSpec((FFN_TM, D_MODEL), lambda i: (i, 0)),
        out_shape=jax.ShapeDtypeStruct((t, D_MODEL), F32),
        compiler_params=pltpu.CompilerParams(
            dimension_semantics=("parallel",), vmem_limit_bytes=VMEM_LIMIT),
        name="ffn",
    )(x, pre_g, w_gu, w_down, post_g)


def _norm_proj_body(x_ref, g_ref, w_ref, o_ref):
    hn = _rms(x_ref[...], g_ref[...], NORM_EPS).astype(BF16)
    o_ref[...] = _dot(hn, w_ref[...]).astype(o_ref.dtype)


def _norm_proj(x, g, w, tm):
    t, n = x.shape[0], w.shape[1]
    return pl.pallas_call(
        _norm_proj_body,
        grid=(t // tm,),
        in_specs=[
            pl.BlockSpec((tm, D_MODEL), lambda i: (i, 0)),
            _const_spec((1, D_MODEL)),
            _const_spec((D_MODEL, n)),
        ],
        out_specs=pl.BlockSpec((tm, n), lambda i: (i, 0)),
        out_shape=jax.ShapeDtypeStruct((t, n), BF16),
        compiler_params=pltpu.CompilerParams(
            dimension_semantics=("parallel",), vmem_limit_bytes=VMEM_LIMIT),
        name="norm_proj",
    )(x, g, w)


def _in_proj_body(q_scale, x_ref, g_ref, w_ref, wt_ref, o_ref, dt_ref, hn_ref):
    j = pl.program_id(1)
    tn = o_ref.shape[1]

    @pl.when(j == 0)
    def _():
        hn_ref[...] = _rms(x_ref[...], g_ref[...], NORM_EPS).astype(BF16)

    @pl.when(j < PROJ_MAIN_TILES)
    def _():
        col = j * tn + lax.broadcasted_iota(jnp.int32, (1, tn), 1)
        scale = jnp.where(col < DA_QK, q_scale, 1.0)
        o_ref[...] = (_dot(hn_ref[...], w_ref[...]) * scale).astype(o_ref.dtype)

    @pl.when(j == PROJ_MAIN_TILES)
    def _():
        tail = _dot(hn_ref[...], wt_ref[...])
        dt_ref[...] = tail[:, :LANES]
        o_ref[...] = tail[:, SSD_HEADS:SSD_HEADS + tn].astype(o_ref.dtype)


def _in_proj(x, g, w_main, w_tail, q_scale):
    t = x.shape[0]
    tm, tn = PROJ_TM, PROJ_TN
    assert w_main.shape[1] == PROJ_MAIN_TILES * tn and w_tail.shape[1] == SSD_HEADS + tn
    return pl.pallas_call(
        functools.partial(_in_proj_body, q_scale),
        grid=(t // tm, PROJ_MAIN_TILES + 1),
        in_specs=[
            pl.BlockSpec((tm, D_MODEL), lambda i, j: (i, 0)),
            pl.BlockSpec((1, D_MODEL), lambda i, j: (0, 0)),
            pl.BlockSpec((D_MODEL, tn), lambda i, j: (0, jnp.minimum(j, PROJ_MAIN_TILES - 1))),
            _const_spec((D_MODEL, SSD_HEADS + tn)),
        ],
        out_specs=[
            pl.BlockSpec((tm, tn), lambda i, j: (i, j)),
            pl.BlockSpec((tm, LANES), lambda i, j: (i, 0)),
        ],
        out_shape=[
            jax.ShapeDtypeStruct((t, PROJ_WIDTH), BF16),
            jax.ShapeDtypeStruct((t, LANES), F32),
        ],
        scratch_shapes=[pltpu.VMEM((tm, D_MODEL), BF16)],
        compiler_params=pltpu.CompilerParams(
            dimension_semantics=("parallel", "arbitrary"), vmem_limit_bytes=VMEM_LIMIT),
        name="in_proj",
    )(x, g, w_main, w_tail)


def _attn_body(lam_init, seq, q_ref, k_ref, v_ref, lam_ref, g_ref, o_ref,
               qq_ref, m_ref, acc_ref, vx_ref):
    tq = ATTN_TQ
    heads = range(ATTN_HPS)
    lamv = lam_ref[...]
    lam = (jnp.exp(jnp.sum(lamv[0:1] * lamv[1:2], axis=-1, keepdims=True))
           - jnp.exp(jnp.sum(lamv[2:3] * lamv[3:4], axis=-1, keepdims=True)) + lam_init)
    lane = lax.broadcasted_iota(jnp.int32, (1, DA_PAIR), 1)
    first_half = lane < DA_HEAD_DIM
    ones_col = jnp.broadcast_to(jnp.where(lane == 0, 1.0, 0.0), (seq, DA_PAIR)).astype(BF16)
    for hh in heads:
        vx_ref[hh, :, :DA_PAIR] = v_ref[:, hh * DA_PAIR:(hh + 1) * DA_PAIR]
        vx_ref[hh, :, DA_PAIR:] = ones_col

    def kv_block(k0, width, masked):
        scores = [
            jnp.concatenate(
                [_dot_nt(qq_ref[hh],
                         k_ref[pl.ds(pl.multiple_of(k0 + c, ATTN_KSLAB), ATTN_KSLAB),
                               hh * DA_PAIR:(hh + 1) * DA_PAIR])
                 for c in range(0, width, ATTN_KSLAB)], axis=1)
            for hh in heads]
        if masked:
            row = lax.broadcasted_iota(jnp.int32, (2 * tq, width), 0)
            col = lax.broadcasted_iota(jnp.int32, (2 * tq, width), 1)
            causal = jnp.where(row >= tq, row - tq, row) >= col
            scores = [jnp.where(causal, s, -jnp.inf) for s in scores]
        for hh, s in zip(heads, scores):
            m_old = m_ref[hh]
            m_new = jnp.maximum(m_old, jnp.max(s, axis=-1, keepdims=True))
            alpha = jnp.exp2(m_old - m_new)
            p = jnp.concatenate(
                [jnp.exp2(s[:, c:c + LANES] - m_new).astype(BF16) for c in range(0, width, LANES)],
                axis=1)
            vb = vx_ref[hh, pl.ds(k0, width), :]
            alpha2 = jnp.concatenate([alpha, alpha], axis=1)
            for r in (0, tq):
                acc_ref[hh, r:r + tq, :] = (acc_ref[hh, r:r + tq, :] * alpha2[r:r + tq]
                                            + _dot(p[r:r + tq], vb))
            m_ref[hh] = m_new

    def q_tile(i, carry):
        r0 = pl.multiple_of(i * tq, tq)
        for hh in heads:
            q = q_ref[pl.ds(r0, tq), hh * DA_PAIR:(hh + 1) * DA_PAIR]
            qq_ref[hh, 0:tq, :] = jnp.where(first_half, q, jnp.zeros_like(q))
            qq_ref[hh, tq:2 * tq, :] = jnp.where(first_half, jnp.zeros_like(q), q)
        m_ref[...] = jnp.full(m_ref.shape, -jnp.inf, F32)
        acc_ref[...] = jnp.zeros(acc_ref.shape, F32)

        def double_step(jj, c):
            kv_block(pl.multiple_of(jj * (2 * tq), 2 * tq), 2 * tq, False)
            return c

        lax.fori_loop(0, lax.shift_right_logical(i, 1), double_step, 0)

        @pl.when((i & 1) == 1)
        def _():
            kv_block(pl.multiple_of((i - 1) * tq, tq), tq, False)

        kv_block(r0, tq, True)
        for hh in heads:
            acc = acc_ref[hh]
            o = acc[:, :DA_PAIR] / acc[:, DA_PAIR:DA_PAIR + 1]
            res = o[:tq] - lam * o[tq:]
            out = _rms(res, g_ref[...], SUBLN_EPS) * (1.0 - lam_init)
            o_ref[pl.ds(r0, tq), hh * DA_PAIR:(hh + 1) * DA_PAIR] = out.astype(o_ref.dtype)
        return carry

    lax.fori_loop(0, seq // tq, q_tile, 0)


def _diff_attn(proj, lam_params, subln_g, lam_init, batch, seq):
    t = batch * seq
    tq = ATTN_TQ
    w = ATTN_HPS * DA_PAIR
    steps = DA_HEADS // ATTN_HPS
    return pl.pallas_call(
        functools.partial(_attn_body, lam_init, seq),
        grid=(batch, steps),
        in_specs=[
            pl.BlockSpec((seq, w), lambda b, h: (b, h)),
            pl.BlockSpec((seq, w), lambda b, h: (b, steps + h)),
            pl.BlockSpec((seq, w), lambda b, h: (b, 2 * steps + h)),
            pl.BlockSpec((4, DA_HEAD_DIM), lambda b, h: (0, 0)),
            pl.BlockSpec((1, DA_PAIR), lambda b, h: (0, 0)),
        ],
        out_specs=pl.BlockSpec((seq, w), lambda b, h: (b, h)),
        out_shape=jax.ShapeDtypeStruct((t, DA_V), BF16),
        scratch_shapes=[
            pltpu.VMEM((ATTN_HPS, 2 * tq, DA_PAIR), BF16),
            pltpu.VMEM((ATTN_HPS, 2 * tq, LANES), F32),
            pltpu.VMEM((ATTN_HPS, 2 * tq, 2 * DA_PAIR), F32),
            pltpu.VMEM((ATTN_HPS, seq, 2 * DA_PAIR), BF16),
        ],
        compiler_params=pltpu.CompilerParams(
            dimension_semantics=("parallel", "parallel"), vmem_limit_bytes=VMEM_LIMIT),
        name="diff_attn",
    )(proj, proj, proj, lam_params, subln_g)


def _ssd_body(*refs):
    nz, nx = SSD_INNER // PROJ_PIECE, SSD_CONV_DIM // PROJ_PIECE
    z_refs, xbc_refs = refs[:nz], refs[nz:nz + nx]
    (dt_ref, shift_ref, cw_ref, cb_ref, dtb_ref, alog_ref, dexp_ref, ng_ref,
     y_ref, xe_ref, rt_ref, ybuf_ref) = refs[nz + nx:]
    lc = SSD_CHUNK
    hist = SSD_HIST

    @pl.when(pl.program_id(1) == 0)
    def _():
        xe_ref[0:hist, :] = jnp.zeros((hist, SSD_CONV_DIM), BF16)
        rt_ref[...] = jnp.zeros(rt_ref.shape, F32)

    xb = jnp.concatenate([r[...] for r in xbc_refs], axis=1)
    xe_ref[hist:hist + lc, :] = xb
    shifted = _dot(shift_ref[...], xe_ref[...])
    xe_ref[0:hist, :] = xb[lc - hist:lc, :]
    cw = cw_ref[...]
    conv = cb_ref[...] + cw[SSD_CONV - 1:SSD_CONV, :] * xb.astype(F32)
    for k in range(SSD_CONV - 1):
        conv = conv + cw[k:k + 1, :] * shifted[k * lc:(k + 1) * lc, :]
    act = _silu(conv)

    dt = jax.nn.softplus(dt_ref[...] + dtb_ref[...])
    adt = -jnp.exp(alog_ref[...]) * dt
    ri = lax.broadcasted_iota(jnp.int32, (lc, lc), 0)
    ci = lax.broadcasted_iota(jnp.int32, (lc, lc), 1)
    lower = ri >= ci
    tril = jnp.where(lower, 1.0, 0.0)
    acs = jnp.dot(tril, adt, precision=lax.Precision.HIGHEST, preferred_element_type=F32)
    acs_t = acs.T
    dt_t = dt.T
    tot_t = acs_t[:, lc - 1:lc]
    w_t = jnp.exp(tot_t - acs_t) * dt_t
    etot_t = jnp.exp(tot_t)
    src_t = acs_t - jnp.log(dt_t)

    lane = lax.broadcasted_iota(jnp.int32, (lc, LANES), 1)
    left = lane < SSD_HEAD_DIM
    zero16 = jnp.zeros((lc, LANES), BF16)

    for g in range(SSD_GROUPS):
        b0 = SSD_INNER + g * SSD_STATE
        c0 = SSD_INNER + SSD_GROUPS * SSD_STATE + g * SSD_STATE
        cg = act[:, c0:c0 + SSD_STATE]
        bg_t = act[:, b0:b0 + SSD_STATE].T
        cbg = _dot(cg.astype(BF16), bg_t.astype(BF16))
        for pr in range(SSD_HEADS_PER_GROUP // 2):
            h0 = g * SSD_HEADS_PER_GROUP + 2 * pr
            x0 = h0 * SSD_HEAD_DIM
            xpair = act[:, x0:x0 + LANES]
            rtpair = rt_ref[:, x0:x0 + LANES]
            x16 = xpair.astype(BF16)
            r16 = rtpair.astype(BF16)
            xl, xr = jnp.where(left, x16, zero16), jnp.where(left, zero16, x16)
            rl, rr = jnp.where(left, r16, zero16), jnp.where(left, zero16, r16)
            mh, ech, wh = [], [], []
            for h in (h0, h0 + 1):
                bc = jnp.broadcast_to(acs[:, h:h + 1], (lc, lc))
                seg = jnp.where(lower, bc - src_t[h:h + 1, :], -jnp.inf)
                mh.append((cbg * jnp.exp(seg)).astype(BF16))
                ech.append((cg * jnp.exp(bc)).astype(BF16))
                wh.append((bg_t * w_t[h:h + 1, :]).astype(BF16))
            ypair = _dot(jnp.concatenate(mh + ech, axis=1),
                         jnp.concatenate([xl, xr, rl, rr], axis=0))
            spair = _dot(jnp.concatenate(wh, axis=1), jnp.concatenate([xl, xr], axis=0))
            e0 = jnp.broadcast_to(etot_t[h0:h0 + 1, :], (1, LANES))
            e1 = jnp.broadcast_to(etot_t[h0 + 1:h0 + 2, :], (1, LANES))
            rt_ref[:, x0:x0 + LANES] = rtpair * jnp.where(left[0:1], e0, e1) + spair
            yp = ypair + dexp_ref[:, x0:x0 + LANES] * xpair
            zc = x0 % PROJ_PIECE
            zpair = z_refs[x0 // PROJ_PIECE][:, zc:zc + LANES].astype(F32)
            ybuf_ref[:, x0:x0 + LANES] = yp * _silu(zpair)

    for g in range(SSD_GROUPS):
        g0 = g * SSD_GROUP_W
        yg = ybuf_ref[:, g0:g0 + SSD_GROUP_W]
        y_ref[:, g0:g0 + SSD_GROUP_W] = _rms(
            yg, ng_ref[:, g0:g0 + SSD_GROUP_W], SUBLN_EPS).astype(y_ref.dtype)


def _ssd(proj, dt_raw, conv_w, conv_b, dt_bias, a_log, d_exp, norm_g, batch, seq):
    t = batch * seq
    lc = SSD_CHUNK
    nc = seq // lc
    row = lambda b, c: b * nc + c
    t_idx = jnp.arange((SSD_CONV - 1) * lc)
    src = SSD_HIST + t_idx % lc - (SSD_CONV - 1) + t_idx // lc
    shift = (src[:, None] == jnp.arange(SSD_HIST + lc)[None, :]).astype(BF16)

    def pieces(off, width):
        return [pl.BlockSpec((lc, PROJ_PIECE), functools.partial(
            lambda b, c, blk: (row(b, c), blk), blk=(off + p0) // PROJ_PIECE))
            for p0 in range(0, width, PROJ_PIECE)]

    n_pieces = (SSD_INNER + SSD_CONV_DIM) // PROJ_PIECE
    return pl.pallas_call(
        _ssd_body,
        grid=(batch, nc),
        in_specs=pieces(PROJ_Z_OFF, SSD_INNER) + pieces(PROJ_XBC_OFF, SSD_CONV_DIM) + [
            pl.BlockSpec((lc, LANES), lambda b, c: (row(b, c), 0)),
            pl.BlockSpec(((SSD_CONV - 1) * lc, SSD_HIST + lc), lambda b, c: (0, 0)),
            pl.BlockSpec((SSD_CONV, SSD_CONV_DIM), lambda b, c: (0, 0)),
            pl.BlockSpec((1, SSD_CONV_DIM), lambda b, c: (0, 0)),
            pl.BlockSpec((1, LANES), lambda b, c: (0, 0)),
            pl.BlockSpec((1, LANES), lambda b, c: (0, 0)),
            pl.BlockSpec((1, SSD_INNER), lambda b, c: (0, 0)),
            pl.BlockSpec((1, SSD_INNER), lambda b, c: (0, 0)),
        ],
        out_specs=pl.BlockSpec((lc, SSD_INNER), lambda b, c: (row(b, c), 0)),
        out_shape=jax.ShapeDtypeStruct((t, SSD_INNER), BF16),
        scratch_shapes=[
            pltpu.VMEM((SSD_HIST + lc, SSD_CONV_DIM), BF16),
            pltpu.VMEM((SSD_STATE, SSD_INNER), F32),
            pltpu.VMEM((lc, SSD_INNER), F32),
        ],
        compiler_params=pltpu.CompilerParams(
            dimension_semantics=("parallel", "arbitrary"), vmem_limit_bytes=VMEM_LIMIT),
        name="ssd",
    )(*([proj] * n_pieces), dt_raw, shift, conv_w, conv_b, dt_bias, a_log, d_exp, norm_g)


def _mix_xattn_body(ao_ref, ys_ref, gl_ref, bg_ref, x_ref, wa_ref, ws_ref, wm_ref, mpost_ref,
                    xpre_ref, wq_ref, kv_ref, wo_ref, xpost_ref, o_ref):
    attn_out = _dot(ao_ref[...], wa_ref[...])
    ssd_out = _dot(ys_ref[...], ws_ref[...])
    gates = jax.nn.sigmoid(gl_ref[...].astype(F32) + bg_ref[...])
    mixed = gates[:, :D_MODEL] * attn_out + gates[:, D_MODEL:] * ssd_out
    mixed = _dot(mixed.astype(BF16), wm_ref[...])
    x = x_ref[...] + _rms(mixed, mpost_ref[...], NORM_EPS)

    hq = _rms(x, xpre_ref[...], NORM_EPS).astype(BF16)
    qx = (_dot(hq, wq_ref[...]) * (XA_HEAD_DIM ** -0.5)).astype(BF16)
    heads = []
    for h in range(XA_HEADS):
        c0 = h * XA_HEAD_DIM
        s = _dot_nt(qx[:, c0:c0 + XA_HEAD_DIM], kv_ref[:, c0:c0 + XA_HEAD_DIM])
        e = jnp.exp(s - jnp.max(s, axis=-1, keepdims=True))
        denom = jnp.sum(e, axis=-1, keepdims=True)
        oh = _dot(e.astype(BF16), kv_ref[:, D_MODEL + c0:D_MODEL + c0 + XA_HEAD_DIM])
        heads.append((oh / denom).astype(BF16))
    xo = _dot(jnp.concatenate(heads, axis=1), wo_ref[...])
    o_ref[...] = x + _rms(xo, xpost_ref[...], NORM_EPS)


def _mix_xattn(attn_o, y_ssd, proj, b_gate, x, w_attn, w_ssd, w_mix, mix_post_g,
               xa_pre_g, w_q, kv, w_o, xa_post_g, seq):
    t = x.shape[0]
    tm = MIX_TM
    gw = N_BRANCH * D_MODEL
    per_batch = seq // tm
    row_tile = lambda w: pl.BlockSpec((tm, w), lambda i: (i, 0))
    return pl.pallas_call(
        _mix_xattn_body,
        grid=(t // tm,),
        in_specs=[
            row_tile(DA_V),
            row_tile(SSD_INNER),
            pl.BlockSpec((tm, gw), lambda i: (i, PROJ_GATE_OFF // gw)),
            _const_spec((1, gw)),
            row_tile(D_MODEL),
            _const_spec((DA_V, D_MODEL)),
            _const_spec((SSD_INNER, D_MODEL)),
            _const_spec((D_MODEL, D_MODEL)),
            _const_spec((1, D_MODEL)),
            _const_spec((1, D_MODEL)),
            _const_spec((D_MODEL, D_MODEL)),
            pl.BlockSpec((MEM_LEN, 2 * D_MODEL), lambda i: (i // per_batch, 0)),
            _const_spec((D_MODEL, D_MODEL)),
            _const_spec((1, D_MODEL)),
        ],
        out_specs=row_tile(D_MODEL),
        out_shape=jax.ShapeDtypeStruct((t, D_MODEL), F32),
        compiler_params=pltpu.CompilerParams(
            dimension_semantics=("parallel",), vmem_limit_bytes=VMEM_LIMIT),
        name="mix_xattn",
    )(attn_o, y_ssd, proj, b_gate, x, w_attn, w_ssd, w_mix, mix_post_g,
      xa_pre_g, w_q, kv, w_o, xa_post_g)


def _row(v):
    return v.reshape(1, -1)


def _pad_lanes(v):
    return jnp.pad(v, ((0, 0), (0, LANES - v.shape[1])))


def _layer(x, mem, layer_idx, batch, seq,
           ffn1_pre_g, ffn1_post_g, ffn1_w_gu, ffn1_w_down,
           mix_pre_g, mix_post_g, w_in, b_gate,
           da_lambda_q1, da_lambda_k1, da_lambda_q2, da_lambda_k2, da_subln_g,
           ssd_conv_w, ssd_conv_b, ssd_dt_bias, ssd_A_log, ssd_D, ssd_norm_g,
           w_branch_attn, w_branch_ssd, w_mix_out,
           xa_pre_g, xa_post_g, mem_norm_g, xa_w_q, xa_w_kv, xa_w_o,
           ffn2_pre_g, ffn2_post_g, ffn2_w_gu, ffn2_w_down):
    x = _ffn(x, _row(ffn1_pre_g), ffn1_w_gu.astype(BF16), ffn1_w_down.astype(BF16),
             _row(ffn1_post_g))

    q_scale = DA_HEAD_DIM ** -0.5 * math.log2(math.e)
    proj, dt_raw = _in_proj(x, _row(mix_pre_g), w_in[:, :PROJ_GATE_OFF].astype(BF16),
                            w_in[:, PROJ_GATE_OFF:].astype(BF16), q_scale)

    lam_init = 0.8 - 0.6 * math.exp(-0.3 * layer_idx)
    lam_params = jnp.stack([da_lambda_q1, da_lambda_k1, da_lambda_q2, da_lambda_k2])
    attn_o = _diff_attn(proj, lam_params, _row(da_subln_g), lam_init, batch, seq)

    y_ssd = _ssd(proj, dt_raw, ssd_conv_w, _row(ssd_conv_b), _pad_lanes(_row(ssd_dt_bias)),
                 _pad_lanes(_row(ssd_A_log)), _row(jnp.repeat(ssd_D, SSD_HEAD_DIM)),
                 _row(ssd_norm_g), batch, seq)

    mem2 = mem.reshape(batch * MEM_LEN, D_MODEL)
    kv = _norm_proj(mem2, _row(mem_norm_g), xa_w_kv.astype(BF16), MEM_LEN)
    x = _mix_xattn(attn_o, y_ssd, proj, _row(b_gate), x, w_branch_attn.astype(BF16),
                   w_branch_ssd.astype(BF16), w_mix_out.astype(BF16), _row(mix_post_g),
                   _row(xa_pre_g), xa_w_q.astype(BF16), kv, xa_w_o.astype(BF16),
                   _row(xa_post_g), seq)

    x = _ffn(x, _row(ffn2_pre_g), ffn2_w_gu.astype(BF16), ffn2_w_down.astype(BF16),
             _row(ffn2_post_g))
    return x


def kernel(x, mem, ffn1_pre_g, ffn1_post_g, ffn1_w_gu, ffn1_w_down, mix_pre_g, mix_post_g, w_in, b_gate, da_lambda_q1, da_lambda_k1, da_lambda_q2, da_lambda_k2, da_subln_g, ssd_conv_w, ssd_conv_b, ssd_dt_bias, ssd_A_log, ssd_D, ssd_norm_g, w_branch_attn, w_branch_ssd, w_mix_out, xa_pre_g, xa_post_g, mem_norm_g, xa_w_q, xa_w_kv, xa_w_o, ffn2_pre_g, ffn2_post_g, ffn2_w_gu, ffn2_w_down):
    batch, seq, d = x.shape
    params = (ffn1_pre_g, ffn1_post_g, ffn1_w_gu, ffn1_w_down, mix_pre_g, mix_post_g, w_in, b_gate,
              da_lambda_q1, da_lambda_k1, da_lambda_q2, da_lambda_k2, da_subln_g,
              ssd_conv_w, ssd_conv_b, ssd_dt_bias, ssd_A_log, ssd_D, ssd_norm_g,
              w_branch_attn, w_branch_ssd, w_mix_out,
              xa_pre_g, xa_post_g, mem_norm_g, xa_w_q, xa_w_kv, xa_w_o,
              ffn2_pre_g, ffn2_post_g, ffn2_w_gu, ffn2_w_down)
    h = x.reshape(batch * seq, d)
    for layer in range(ffn1_pre_g.shape[0]):
        h = _layer(h, mem, layer, batch, seq, *[p[layer] for p in params])
    return h.reshape(batch, seq, d)
```

```python
import functools
import math

import jax
import jax.numpy as jnp
from jax import lax
from jax.experimental import pallas as pl
from jax.experimental.pallas import tpu as pltpu

F32 = jnp.float32
BF16 = jnp.bfloat16

D_MODEL = 1024
MEM_LEN = 256
DA_HEADS = 8
DA_HEAD_DIM = 64
DA_PAIR = 2 * DA_HEAD_DIM
DA_QK = DA_HEADS * DA_PAIR
DA_V = DA_HEADS * DA_PAIR
SSD_INNER = 2 * D_MODEL
SSD_HEAD_DIM = 64
SSD_HEADS = SSD_INNER // SSD_HEAD_DIM
SSD_GROUPS = 4
SSD_HEADS_PER_GROUP = SSD_HEADS // SSD_GROUPS
SSD_STATE = 128
SSD_CONV = 4
SSD_CHUNK = 128
SSD_GROUP_W = SSD_INNER // SSD_GROUPS
SSD_HIST = 16
SSD_CONV_DIM = SSD_INNER + 2 * SSD_GROUPS * SSD_STATE
XA_HEADS = 4
XA_HEAD_DIM = D_MODEL // XA_HEADS
D_FF = 2816
N_BRANCH = 2
NORM_EPS = 1e-6
SUBLN_EPS = 1e-5

LANES = 128
SUBLANES = 8
VMEM_LIMIT = 56 * 1024 * 1024

PROJ_Z_OFF = 3 * DA_QK
PROJ_XBC_OFF = PROJ_Z_OFF + SSD_INNER
PROJ_GATE_OFF = PROJ_XBC_OFF + SSD_CONV_DIM
PROJ_WIDTH = PROJ_GATE_OFF + N_BRANCH * D_MODEL
PROJ_PIECE = 1024

FFN_TM = 1024
FFN_CHUNK = 256
PROJ_TM = 1024
PROJ_TN = 2048
PROJ_MAIN_TILES = PROJ_GATE_OFF // PROJ_TN
ATTN_TQ = 512
ATTN_KSLAB = 256
ATTN_HPS = 2
MIX_TM = 512


def _rms(x, g, eps):
    return x * lax.rsqrt(jnp.mean(x * x, axis=-1, keepdims=True) + eps) * g


def _silu(x):
    h = 0.5 * x
    return h + h * jnp.tanh(h)


def _dot(a, b):
    return jnp.dot(a, b, preferred_element_type=F32)


def _dot_nt(a, b):
    return lax.dot_general(a, b, (((1,), (1,)), ((), ())), preferred_element_type=F32)


def _const_spec(shape):
    nd = len(shape)
    return pl.BlockSpec(shape, lambda *_: (0,) * nd, pipeline_mode=pl.Buffered(1))


def _ffn_body(x_ref, pre_ref, wgu_ref, wd_ref, post_ref, o_ref):
    x = x_ref[...]
    hn = _rms(x, pre_ref[...], NORM_EPS).astype(BF16)
    acc = None
    for c0 in range(0, D_FF, FFN_CHUNK):
        g = _dot(hn, wgu_ref[:, c0:c0 + FFN_CHUNK])
        u = _dot(hn, wgu_ref[:, D_FF + c0:D_FF + c0 + FFN_CHUNK])
        a = (_silu(g) * u).astype(BF16)
        d = _dot(a, wd_ref[c0:c0 + FFN_CHUNK, :])
        acc = d if acc is None else acc + d
    o_ref[...] = x + 0.5 * _rms(acc, post_ref[...], NORM_EPS)


def _ffn(x, pre_g, w_gu, w_down, post_g):
    t = x.shape[0]
    return pl.pallas_call(
        _ffn_body,
        grid=(t // FFN_TM,),
        in_specs=[
            pl.BlockSpec((FFN_TM, D_MODEL), lambda i: (i, 0)),
            _const_spec((1, D_MODEL)),
            _const_spec((D_MODEL, 2 * D_FF)),
            _const_spec((D_FF, D_MODEL)),
            _const_spec((1, D_MODEL)),
        ],
        out_specs=pl.BlockSpec((FFN_TM, D_MODEL), lambda i: (i, 0)),
        out_shape=jax.ShapeDtypeStruct((t, D_MODEL), F32),
        compiler_params=pltpu.CompilerParams(
            dimension_semantics=("parallel",), vmem_limit_bytes=VMEM_LIMIT),
        name="ffn",
    )(x, pre_g, w_gu, w_down, post_g)


def _norm_proj_body(x_ref, g_ref, w_ref, o_ref):
    hn = _rms(x_ref[...], g_ref[...], NORM_EPS).astype(BF16)
    o_ref[...] = _dot(hn, w_ref[...]).astype(o_ref.dtype)


def _norm_proj(x, g, w, tm):
    t, n = x.shape[0], w.shape[1]
    return pl.pallas_call(
        _norm_proj_body,
        grid=(t // tm,),
        in_specs=[
            pl.BlockSpec((tm, D_MODEL), lambda i: (i, 0)),
            _const_spec((1, D_MODEL)),
            _const_spec((D_MODEL, n)),
        ],
        out_specs=pl.BlockSpec((tm, n), lambda i: (i, 0)),
        out_shape=jax.ShapeDtypeStruct((t, n), BF16),
        compiler_params=pltpu.CompilerParams(
            dimension_semantics=("parallel",), vmem_limit_bytes=VMEM_LIMIT),
        name="norm_proj",
    )(x, g, w)


def _in_proj_body(q_scale, x_ref, g_ref, w_ref, wt_ref, o_ref, dt_ref, hn_ref):
    j = pl.program_id(1)
    tn = o_ref.shape[1]

    @pl.when(j == 0)
    def _():
        hn_ref[...] = _rms(x_ref[...], g_ref[...], NORM_EPS).astype(BF16)

    @pl.when(j < PROJ_MAIN_TILES)
    def _():
        col = j * tn + lax.broadcasted_iota(jnp.int32, (1, tn), 1)
        scale = jnp.where(col < DA_QK, q_scale, 1.0)
        o_ref[...] = (_dot_nt(hn_ref[...], w_ref[...]) * scale).astype(o_ref.dtype)

    @pl.when(j == PROJ_MAIN_TILES)
    def _():
        tail = _dot_nt(hn_ref[...], wt_ref[...])
        dt_ref[...] = tail[:, :LANES]
        o_ref[...] = tail[:, SSD_HEADS:SSD_HEADS + tn].astype(o_ref.dtype)


def _in_proj(x, g, w_t, q_scale):
    t = x.shape[0]
    tm, tn = PROJ_TM, PROJ_TN
    assert w_t.shape[0] == PROJ_MAIN_TILES * tn + SSD_HEADS + tn
    return pl.pallas_call(
        functools.partial(_in_proj_body, q_scale),
        grid=(t // tm, PROJ_MAIN_TILES + 1),
        in_specs=[
            pl.BlockSpec((tm, D_MODEL), lambda i, j: (i, 0)),
            pl.BlockSpec((1, D_MODEL), lambda i, j: (0, 0)),
            pl.BlockSpec((tn, D_MODEL), lambda i, j: (jnp.minimum(j, PROJ_MAIN_TILES - 1), 0)),
            pl.BlockSpec((pl.Element(SSD_HEADS + tn), pl.Element(D_MODEL)),
                         lambda i, j: (PROJ_MAIN_TILES * tn, 0), pipeline_mode=pl.Buffered(1)),
        ],
        out_specs=[
            pl.BlockSpec((tm, tn), lambda i, j: (i, j)),
            pl.BlockSpec((tm, LANES), lambda i, j: (i, 0)),
        ],
        out_shape=[
            jax.ShapeDtypeStruct((t, PROJ_WIDTH), BF16),
            jax.ShapeDtypeStruct((t, LANES), F32),
        ],
        scratch_shapes=[pltpu.VMEM((tm, D_MODEL), BF16)],
        compiler_params=pltpu.CompilerParams(
            dimension_semantics=("parallel", "arbitrary"), vmem_limit_bytes=VMEM_LIMIT),
        name="in_proj",
    )(x, g, w_t, w_t)


def _attn_body(lam_init, seq, q_ref, k_ref, v_ref, lam_ref, g_ref, o_ref,
               qq_ref, m_ref, acc_ref, vx_ref, mask_ref):
    tq = ATTN_TQ
    heads = range(ATTN_HPS)
    lamv = lam_ref[...]
    lam = (jnp.exp(jnp.sum(lamv[0:1] * lamv[1:2], axis=-1, keepdims=True))
           - jnp.exp(jnp.sum(lamv[2:3] * lamv[3:4], axis=-1, keepdims=True)) + lam_init)
    lane = lax.broadcasted_iota(jnp.int32, (1, DA_PAIR), 1)
    first_half = lane < DA_HEAD_DIM
    ones_col = jnp.broadcast_to(jnp.where(lane == 0, 1.0, 0.0), (seq, DA_PAIR)).astype(BF16)
    for hh in heads:
        vx_ref[hh, :, :DA_PAIR] = v_ref[:, hh * DA_PAIR:(hh + 1) * DA_PAIR]
        vx_ref[hh, :, DA_PAIR:] = ones_col

    row = lax.broadcasted_iota(jnp.int32, (2 * tq, tq), 0)
    col = lax.broadcasted_iota(jnp.int32, (2 * tq, tq), 1)
    mask_ref[...] = jnp.where(jnp.where(row >= tq, row - tq, row) >= col, 0.0, -jnp.inf)

    def kv_block(k0, width, first):
        scores = [
            jnp.concatenate(
                [_dot_nt(qq_ref[hh],
                         k_ref[pl.ds(pl.multiple_of(k0 + c, ATTN_KSLAB), ATTN_KSLAB),
                               hh * DA_PAIR:(hh + 1) * DA_PAIR])
                 for c in range(0, width, ATTN_KSLAB)], axis=1)
            for hh in heads]
        if first:
            scores = [s + mask_ref[...] for s in scores]
        for hh, s in zip(heads, scores):
            m_new = jnp.broadcast_to(jnp.max(s, axis=-1, keepdims=True), (2 * tq, LANES))
            if not first:
                m_old = m_ref[hh]
                m_new = jnp.maximum(m_old, m_new)
                alpha = jnp.exp2(m_old - m_new)
                alpha2 = jnp.concatenate([alpha, alpha], axis=1)
            p = jnp.concatenate(
                [jnp.exp2(s[:, c:c + LANES] - m_new).astype(BF16) for c in range(0, width, LANES)],
                axis=1)
            vb = vx_ref[hh, pl.ds(k0, width), :]
            for r in (0, tq):
                pv = _dot(p[r:r + tq], vb)
                if first:
                    acc_ref[hh, r:r + tq, :] = pv
                else:
                    acc_ref[hh, r:r + tq, :] = acc_ref[hh, r:r + tq, :] * alpha2[r:r + tq] + pv
            m_ref[hh] = m_new

    def q_tile(i, carry):
        r0 = pl.multiple_of(i * tq, tq)
        for hh in heads:
            q = q_ref[pl.ds(r0, tq), hh * DA_PAIR:(hh + 1) * DA_PAIR]
            qq_ref[hh, 0:tq, :] = jnp.where(first_half, q, jnp.zeros_like(q))
            qq_ref[hh, tq:2 * tq, :] = jnp.where(first_half, jnp.zeros_like(q), q)

        kv_block(r0, tq, True)

        def double_step(jj, c):
            kv_block(pl.multiple_of(jj * (2 * tq), 2 * tq), 2 * tq, False)
            return c

        lax.fori_loop(0, lax.shift_right_logical(i, 1), double_step, 0)

        @pl.when((i & 1) == 1)
        def _():
            kv_block(pl.multiple_of((i - 1) * tq, tq), tq, False)

        for hh in heads:
            acc = acc_ref[hh]
            o = acc[:, :DA_PAIR] / acc[:, DA_PAIR:DA_PAIR + 1]
            res = o[:tq] - lam * o[tq:]
            out = _rms(res, g_ref[...], SUBLN_EPS) * (1.0 - lam_init)
            o_ref[pl.ds(r0, tq), hh * DA_PAIR:(hh + 1) * DA_PAIR] = out.astype(o_ref.dtype)
        return carry

    lax.fori_loop(0, seq // tq, q_tile, 0)


def _diff_attn(proj, lam_params, subln_g, lam_init, batch, seq):
    t = batch * seq
    tq = ATTN_TQ
    w = ATTN_HPS * DA_PAIR
    steps = DA_HEADS // ATTN_HPS
    return pl.pallas_call(
        functools.partial(_attn_body, lam_init, seq),
        grid=(batch, steps),
        in_specs=[
            pl.BlockSpec((seq, w), lambda b, h: (b, h)),
            pl.BlockSpec((seq, w), lambda b, h: (b, steps + h)),
            pl.BlockSpec((seq, w), lambda b, h: (b, 2 * steps + h)),
            pl.BlockSpec((4, DA_HEAD_DIM), lambda b, h: (0, 0)),
            pl.BlockSpec((1, DA_PAIR), lambda b, h: (0, 0)),
        ],
        out_specs=pl.BlockSpec((seq, w), lambda b, h: (b, h)),
        out_shape=jax.ShapeDtypeStruct((t, DA_V), BF16),
        scratch_shapes=[
            pltpu.VMEM((ATTN_HPS, 2 * tq, DA_PAIR), BF16),
            pltpu.VMEM((ATTN_HPS, 2 * tq, LANES), F32),
            pltpu.VMEM((ATTN_HPS, 2 * tq, 2 * DA_PAIR), F32),
            pltpu.VMEM((ATTN_HPS, seq, 2 * DA_PAIR), BF16),
            pltpu.VMEM((2 * tq, tq), F32),
        ],
        compiler_params=pltpu.CompilerParams(
            dimension_semantics=("parallel", "parallel"), vmem_limit_bytes=VMEM_LIMIT),
        name="diff_attn",
    )(proj, proj, proj, lam_params, subln_g)


def _ssd_body(*refs):
    nz, nx = SSD_INNER // PROJ_PIECE, SSD_CONV_DIM // PROJ_PIECE
    z_refs, xbc_refs = refs[:nz], refs[nz:nz + nx]
    (dt_ref, shift_ref, cw_ref, cb_ref, dtb_ref, alog_ref, dexp_ref, ng_ref,
     y_ref, xe_ref, rt_ref, ybuf_ref) = refs[nz + nx:]
    lc = SSD_CHUNK
    hist = SSD_HIST

    @pl.when(pl.program_id(1) == 0)
    def _():
        xe_ref[0:hist, :] = jnp.zeros((hist, SSD_CONV_DIM), BF16)
        rt_ref[...] = jnp.zeros(rt_ref.shape, F32)

    xb = jnp.concatenate([r[...] for r in xbc_refs], axis=1)
    xe_ref[hist:hist + lc, :] = xb
    xe = xe_ref[...]
    xe_ref[0:hist, :] = xb[lc - hist:lc, :]
    cw = cw_ref[...].astype(BF16)
    taps = jnp.concatenate([xe * cw[k:k + 1, :] for k in range(SSD_CONV)], axis=0)
    act = _silu(_dot(shift_ref[...], taps) + cb_ref[...])

    dt = jax.nn.softplus(dt_ref[...] + dtb_ref[...])
    adt = -jnp.exp(alog_ref[...]) * dt
    ri = lax.broadcasted_iota(jnp.int32, (lc, lc), 0)
    ci = lax.broadcasted_iota(jnp.int32, (lc, lc), 1)
    lower = ri >= ci
    tril = jnp.where(lower, 1.0, 0.0)
    acs = jnp.dot(tril, adt, precision=lax.Precision.HIGHEST, preferred_element_type=F32)
    acs_t = acs.T
    dt_t = dt.T
    tot_t = acs_t[:, lc - 1:lc]
    w_t = jnp.exp(tot_t - acs_t) * dt_t
    etot_t = jnp.exp(tot_t)
    src_t = acs_t - jnp.log(dt_t)

    lane = lax.broadcasted_iota(jnp.int32, (lc, LANES), 1)
    left = lane < SSD_HEAD_DIM
    zero16 = jnp.zeros((lc, LANES), BF16)

    for g in range(SSD_GROUPS):
        b0 = SSD_INNER + g * SSD_STATE
        c0 = SSD_INNER + SSD_GROUPS * SSD_STATE + g * SSD_STATE
        cg = act[:, c0:c0 + SSD_STATE]
        bg_t = act[:, b0:b0 + SSD_STATE].T
        cbg = _dot(cg.astype(BF16), bg_t.astype(BF16))
        for pr in range(SSD_HEADS_PER_GROUP // 2):
            h0 = g * SSD_HEADS_PER_GROUP + 2 * pr
            x0 = h0 * SSD_HEAD_DIM
            xpair = act[:, x0:x0 + LANES]
            rtpair = rt_ref[:, x0:x0 + LANES]
            x16 = xpair.astype(BF16)
            r16 = rtpair.astype(BF16)
            xl, xr = jnp.where(left, x16, zero16), jnp.where(left, zero16, x16)
            rl, rr = jnp.where(left, r16, zero16), jnp.where(left, zero16, r16)
            mh, ech, wh = [], [], []
            for h in (h0, h0 + 1):
                bc = jnp.broadcast_to(acs[:, h:h + 1], (lc, lc))
                seg = jnp.where(lower, bc - src_t[h:h + 1, :], -jnp.inf)
                mh.append((cbg * jnp.exp(seg)).astype(BF16))
                ech.append((cg * jnp.exp(bc)).astype(BF16))
                wh.append((bg_t * w_t[h:h + 1, :]).astype(BF16))
            ypair = _dot(jnp.concatenate(mh + ech, axis=1),
                         jnp.concatenate([xl, xr, rl, rr], axis=0))
            spair = _dot(jnp.concatenate(wh, axis=1), jnp.concatenate([xl, xr], axis=0))
            e0 = jnp.broadcast_to(etot_t[h0:h0 + 1, :], (1, LANES))
            e1 = jnp.broadcast_to(etot_t[h0 + 1:h0 + 2, :], (1, LANES))
            rt_ref[:, x0:x0 + LANES] = rtpair * jnp.where(left[0:1], e0, e1) + spair
            yp = ypair + dexp_ref[:, x0:x0 + LANES] * xpair
            zc = x0 % PROJ_PIECE
            zpair = z_refs[x0 // PROJ_PIECE][:, zc:zc + LANES].astype(F32)
            ybuf_ref[:, x0:x0 + LANES] = yp * _silu(zpair)

    for g in range(SSD_GROUPS):
        g0 = g * SSD_GROUP_W
        yg = ybuf_ref[:, g0:g0 + SSD_GROUP_W]
        y_ref[:, g0:g0 + SSD_GROUP_W] = _rms(
            yg, ng_ref[:, g0:g0 + SSD_GROUP_W], SUBLN_EPS).astype(y_ref.dtype)


def _ssd(proj, dt_raw, conv_w, conv_b, dt_bias, a_log, d_exp, norm_g, batch, seq):
    t = batch * seq
    lc = SSD_CHUNK
    nc = seq // lc
    row = lambda b, c: b * nc + c
    rows_e = SSD_HIST + lc
    col = jnp.arange(SSD_CONV * rows_e)
    src = SSD_HIST + jnp.arange(lc)[:, None] - (SSD_CONV - 1) + col[None, :] // rows_e
    shift = (col[None, :] % rows_e == src).astype(BF16)

    def pieces(off, width):
        return [pl.BlockSpec((lc, PROJ_PIECE), functools.partial(
            lambda b, c, blk: (row(b, c), blk), blk=(off + p0) // PROJ_PIECE))
            for p0 in range(0, width, PROJ_PIECE)]

    n_pieces = (SSD_INNER + SSD_CONV_DIM) // PROJ_PIECE
    return pl.pallas_call(
        _ssd_body,
        grid=(batch, nc),
        in_specs=pieces(PROJ_Z_OFF, SSD_INNER) + pieces(PROJ_XBC_OFF, SSD_CONV_DIM) + [
            pl.BlockSpec((lc, LANES), lambda b, c: (row(b, c), 0)),
            pl.BlockSpec((lc, SSD_CONV * rows_e), lambda b, c: (0, 0)),
            pl.BlockSpec((SSD_CONV, SSD_CONV_DIM), lambda b, c: (0, 0)),
            pl.BlockSpec((1, SSD_CONV_DIM), lambda b, c: (0, 0)),
            pl.BlockSpec((1, LANES), lambda b, c: (0, 0)),
            pl.BlockSpec((1, LANES), lambda b, c: (0, 0)),
            pl.BlockSpec((1, SSD_INNER), lambda b, c: (0, 0)),
            pl.BlockSpec((1, SSD_INNER), lambda b, c: (0, 0)),
        ],
        out_specs=pl.BlockSpec((lc, SSD_INNER), lambda b, c: (row(b, c), 0)),
        out_shape=jax.ShapeDtypeStruct((t, SSD_INNER), BF16),
        scratch_shapes=[
            pltpu.VMEM((SSD_HIST + lc, SSD_CONV_DIM), BF16),
            pltpu.VMEM((SSD_STATE, SSD_INNER), F32),
            pltpu.VMEM((lc, SSD_INNER), F32),
        ],
        compiler_params=pltpu.CompilerParams(
            dimension_semantics=("parallel", "arbitrary"), vmem_limit_bytes=VMEM_LIMIT),
        name="ssd",
    )(*([proj] * n_pieces), dt_raw, shift, conv_w, conv_b, dt_bias, a_log, d_exp, norm_g)


def _mix_xattn_body(ao_ref, ys_ref, gl_ref, bg_ref, x_ref, wa_ref, ws_ref, wm_ref, mpost_ref,
                    xpre_ref, wq_ref, kv_ref, wo_ref, xpost_ref, o_ref):
    attn_out = _dot(ao_ref[...], wa_ref[...])
    ssd_out = _dot(ys_ref[...], ws_ref[...])
    gates = jax.nn.sigmoid(gl_ref[...].astype(F32) + bg_ref[...])
    mixed = gates[:, :D_MODEL] * attn_out + gates[:, D_MODEL:] * ssd_out
    mixed = _dot(mixed.astype(BF16), wm_ref[...])
    x = x_ref[...] + _rms(mixed, mpost_ref[...], NORM_EPS)

    hq = _rms(x, xpre_ref[...], NORM_EPS).astype(BF16)
    qx = (_dot(hq, wq_ref[...]) * (XA_HEAD_DIM ** -0.5)).astype(BF16)
    heads = []
    for h in range(XA_HEADS):
        c0 = h * XA_HEAD_DIM
        s = _dot_nt(qx[:, c0:c0 + XA_HEAD_DIM], kv_ref[:, c0:c0 + XA_HEAD_DIM])
        e = jnp.exp(s - jnp.max(s, axis=-1, keepdims=True))
        denom = jnp.sum(e, axis=-1, keepdims=True)
        oh = _dot(e.astype(BF16), kv_ref[:, D_MODEL + c0:D_MODEL + c0 + XA_HEAD_DIM])
        heads.append((oh / denom).astype(BF16))
    xo = _dot(jnp.concatenate(heads, axis=1), wo_ref[...])
    o_ref[...] = x + _rms(xo, xpost_ref[...], NORM_EPS)


def _mix_xattn(attn_o, y_ssd, proj, b_gate, x, w_attn, w_ssd, w_mix, mix_post_g,
               xa_pre_g, w_q, kv, w_o, xa_post_g, seq):
    t = x.shape[0]
    tm = MIX_TM
    gw = N_BRANCH * D_MODEL
    per_batch = seq // tm
    row_tile = lambda w: pl.BlockSpec((tm, w), lambda i: (i, 0))
    return pl.pallas_call(
        _mix_xattn_body,
        grid=(t // tm,),
        in_specs=[
            row_tile(DA_V),
            row_tile(SSD_INNER),
            pl.BlockSpec((tm, gw), lambda i: (i, PROJ_GATE_OFF // gw)),
            _const_spec((1, gw)),
            row_tile(D_MODEL),
            _const_spec((DA_V, D_MODEL)),
            _const_spec((SSD_INNER, D_MODEL)),
            _const_spec((D_MODEL, D_MODEL)),
            _const_spec((1, D_MODEL)),
            _const_spec((1, D_MODEL)),
            _const_spec((D_MODEL, D_MODEL)),
            pl.BlockSpec((MEM_LEN, 2 * D_MODEL), lambda i: (i // per_batch, 0)),
            _const_spec((D_MODEL, D_MODEL)),
            _const_spec((1, D_MODEL)),
        ],
        out_specs=row_tile(D_MODEL),
        out_shape=jax.ShapeDtypeStruct((t, D_MODEL), F32),
        compiler_params=pltpu.CompilerParams(
            dimension_semantics=("parallel",), vmem_limit_bytes=VMEM_LIMIT),
        name="mix_xattn",
    )(attn_o, y_ssd, proj, b_gate, x, w_attn, w_ssd, w_mix, mix_post_g,
      xa_pre_g, w_q, kv, w_o, xa_post_g)


def _row(v):
    return v.reshape(1, -1)


def _pad_lanes(v):
    return jnp.pad(v, ((0, 0), (0, LANES - v.shape[1])))


def _layer(x, mem, layer_idx, batch, seq,
           ffn1_pre_g, ffn1_post_g, ffn1_w_gu, ffn1_w_down,
           mix_pre_g, mix_post_g, w_in, b_gate,
           da_lambda_q1, da_lambda_k1, da_lambda_q2, da_lambda_k2, da_subln_g,
           ssd_conv_w, ssd_conv_b, ssd_dt_bias, ssd_A_log, ssd_D, ssd_norm_g,
           w_branch_attn, w_branch_ssd, w_mix_out,
           xa_pre_g, xa_post_g, mem_norm_g, xa_w_q, xa_w_kv, xa_w_o,
           ffn2_pre_g, ffn2_post_g, ffn2_w_gu, ffn2_w_down):
    x = _ffn(x, _row(ffn1_pre_g), ffn1_w_gu.astype(BF16), ffn1_w_down.astype(BF16),
             _row(ffn1_post_g))

    q_scale = DA_HEAD_DIM ** -0.5 * math.log2(math.e)
    proj, dt_raw = _in_proj(x, _row(mix_pre_g), w_in.T.astype(BF16), q_scale)

    lam_init = 0.8 - 0.6 * math.exp(-0.3 * layer_idx)
    lam_params = jnp.stack([da_lambda_q1, da_lambda_k1, da_lambda_q2, da_lambda_k2])
    attn_o = _diff_attn(proj, lam_params, _row(da_subln_g), lam_init, batch, seq)

    y_ssd = _ssd(proj, dt_raw, ssd_conv_w, _row(ssd_conv_b), _pad_lanes(_row(ssd_dt_bias)),
                 _pad_lanes(_row(ssd_A_log)), _row(jnp.repeat(ssd_D, SSD_HEAD_DIM)),
                 _row(ssd_norm_g), batch, seq)

    mem2 = mem.reshape(batch * MEM_LEN, D_MODEL)
    kv = _norm_proj(mem2, _row(mem_norm_g), xa_w_kv.astype(BF16), MEM_LEN)
    x = _mix_xattn(attn_o, y_ssd, proj, _row(b_gate), x, w_branch_attn.astype(BF16),
                   w_branch_ssd.astype(BF16), w_mix_out.astype(BF16), _row(mix_post_g),
                   _row(xa_pre_g), xa_w_q.astype(BF16), kv, xa_w_o.astype(BF16),
                   _row(xa_post_g), seq)

    x = _ffn(x, _row(ffn2_pre_g), ffn2_w_gu.astype(BF16), ffn2_w_down.astype(BF16),
             _row(ffn2_post_g))
    return x


def kernel(x, mem, ffn1_pre_g, ffn1_post_g, ffn1_w_gu, ffn1_w_down, mix_pre_g, mix_post_g, w_in, b_gate, da_lambda_q1, da_lambda_k1, da_lambda_q2, da_lambda_k2, da_subln_g, ssd_conv_w, ssd_conv_b, ssd_dt_bias, ssd_A_log, ssd_D, ssd_norm_g, w_branch_attn, w_branch_ssd, w_mix_out, xa_pre_g, xa_post_g, mem_norm_g, xa_w_q, xa_w_kv, xa_w_o, ffn2_pre_g, ffn2_post_g, ffn2_w_gu, ffn2_w_down):
    batch, seq, d = x.shape
    params = (ffn1_pre_g, ffn1_post_g, ffn1_w_gu, ffn1_w_down, mix_pre_g, mix_post_g, w_in, b_gate,
              da_lambda_q1, da_lambda_k1, da_lambda_q2, da_lambda_k2, da_subln_g,
              ssd_conv_w, ssd_conv_b, ssd_dt_bias, ssd_A_log, ssd_D, ssd_norm_g,
              w_branch_attn, w_branch_ssd, w_mix_out,
              xa_pre_g, xa_post_g, mem_norm_g, xa_w_q, xa_w_kv, xa_w_o,
              ffn2_pre_g, ffn2_post_g, ffn2_w_gu, ffn2_w_down)
    h = x.reshape(batch * seq, d)
    for layer in range(ffn1_pre_g.shape[0]):
        h = _layer(h, mem, layer, batch, seq, *[p[layer] for p in params])
    return h.reshape(batch, seq, d)
```

```python
import functools
import math

import jax
import jax.numpy as jnp
from jax import lax
from jax.experimental import pallas as pl
from jax.experimental.pallas import tpu as pltpu

F32 = jnp.float32
BF16 = jnp.bfloat16

D_MODEL = 1024
MEM_LEN = 256
DA_HEADS = 8
DA_HEAD_DIM = 64
DA_PAIR = 2 * DA_HEAD_DIM
DA_QK = DA_HEADS * DA_PAIR
DA_V = DA_HEADS * DA_PAIR
SSD_INNER = 2 * D_MODEL
SSD_HEAD_DIM = 64
SSD_HEADS = SSD_INNER // SSD_HEAD_DIM
SSD_GROUPS = 4
SSD_HEADS_PER_GROUP = SSD_HEADS // SSD_GROUPS
SSD_STATE = 128
SSD_CONV = 4
SSD_CHUNK = 128
SSD_GROUP_W = SSD_INNER // SSD_GROUPS
SSD_SUB = 4
SSD_HIST = 16
SSD_CONV_DIM = SSD_INNER + 2 * SSD_GROUPS * SSD_STATE
XA_HEADS = 4
XA_HEAD_DIM = D_MODEL // XA_HEADS
D_FF = 2816
N_BRANCH = 2
NORM_EPS = 1e-6
SUBLN_EPS = 1e-5

LANES = 128
SUBLANES = 8
VMEM_LIMIT = 56 * 1024 * 1024

PROJ_Z_OFF = 3 * DA_QK
PROJ_XBC_OFF = PROJ_Z_OFF + SSD_INNER
PROJ_GATE_OFF = PROJ_XBC_OFF + SSD_CONV_DIM
PROJ_WIDTH = PROJ_GATE_OFF + N_BRANCH * D_MODEL
PROJ_PIECE = 1024

FFN_TM = 1024
FFN_CHUNK = 256
PROJ_TM = 1024
PROJ_TN = 2048
PROJ_MAIN_TILES = PROJ_GATE_OFF // PROJ_TN
ATTN_TQ = 512
ATTN_KSLAB = 256
ATTN_HPS = 2
MIX_TM = 512


def _rms(x, g, eps):
    return x * lax.rsqrt(jnp.mean(x * x, axis=-1, keepdims=True) + eps) * g


def _silu(x):
    h = 0.5 * x
    return h + h * jnp.tanh(h)


def _dot(a, b):
    return jnp.dot(a, b, preferred_element_type=F32)


def _dot_nt(a, b):
    return lax.dot_general(a, b, (((1,), (1,)), ((), ())), preferred_element_type=F32)


def _const_spec(shape):
    nd = len(shape)
    return pl.BlockSpec(shape, lambda *_: (0,) * nd, pipeline_mode=pl.Buffered(1))


def _ffn_body(x_ref, pre_ref, wgu_ref, wd_ref, post_ref, o_ref):
    x = x_ref[...]
    hn = _rms(x, pre_ref[...], NORM_EPS).astype(BF16)
    acc = None
    for c0 in range(0, D_FF, FFN_CHUNK):
        g = _dot(hn, wgu_ref[:, c0:c0 + FFN_CHUNK])
        u = _dot(hn, wgu_ref[:, D_FF + c0:D_FF + c0 + FFN_CHUNK])
        a = (_silu(g) * u).astype(BF16)
        d = _dot(a, wd_ref[c0:c0 + FFN_CHUNK, :])
        acc = d if acc is None else acc + d
    o_ref[...] = x + 0.5 * _rms(acc, post_ref[...], NORM_EPS)


def _ffn(x, pre_g, w_gu, w_down, post_g):
    t = x.shape[0]
    return pl.pallas_call(
        _ffn_body,
        grid=(t // FFN_TM,),
        in_specs=[
            pl.BlockSpec((FFN_TM, D_MODEL), lambda i: (i, 0)),
            _const_spec((1, D_MODEL)),
            _const_spec((D_MODEL, 2 * D_FF)),
            _const_spec((D_FF, D_MODEL)),
            _const_spec((1, D_MODEL)),
        ],
        out_specs=pl.BlockSpec((FFN_TM, D_MODEL), lambda i: (i, 0)),
        out_shape=jax.ShapeDtypeStruct((t, D_MODEL), F32),
        compiler_params=pltpu.CompilerParams(
            dimension_semantics=("parallel",), vmem_limit_bytes=VMEM_LIMIT),
        name="ffn",
    )(x, pre_g, w_gu, w_down, post_g)


def _norm_proj_body(x_ref, g_ref, w_ref, o_ref):
    hn = _rms(x_ref[...], g_ref[...], NORM_EPS).astype(BF16)
    o_ref[...] = _dot(hn, w_ref[...]).astype(o_ref.dtype)


def _norm_proj(x, g, w, tm):
    t, n = x.shape[0], w.shape[1]
    return pl.pallas_call(
        _norm_proj_body,
        grid=(t // tm,),
        in_specs=[
            pl.BlockSpec((tm, D_MODEL), lambda i: (i, 0)),
            _const_spec((1, D_MODEL)),
            _const_spec((D_MODEL, n)),
        ],
        out_specs=pl.BlockSpec((tm, n), lambda i: (i, 0)),
        out_shape=jax.ShapeDtypeStruct((t, n), BF16),
        compiler_params=pltpu.CompilerParams(
            dimension_semantics=("parallel",), vmem_limit_bytes=VMEM_LIMIT),
        name="norm_proj",
    )(x, g, w)


def _in_proj_body(q_scale, x_ref, g_ref, w_ref, wt_ref, o_ref, dt_ref, hn_ref):
    j = pl.program_id(1)
    tn = o_ref.shape[1]

    @pl.when(j == 0)
    def _():
        hn_ref[...] = _rms(x_ref[...], g_ref[...], NORM_EPS).astype(BF16)

    @pl.when(j < PROJ_MAIN_TILES)
    def _():
        col = j * tn + lax.broadcasted_iota(jnp.int32, (1, tn), 1)
        scale = jnp.where(col < DA_QK, q_scale, 1.0)
        o_ref[...] = (_dot_nt(hn_ref[...], w_ref[...]) * scale).astype(o_ref.dtype)

    @pl.when(j == PROJ_MAIN_TILES)
    def _():
        tail = _dot_nt(hn_ref[...], wt_ref[...])
        dt_ref[...] = tail[:, :LANES]
        o_ref[...] = tail[:, SSD_HEADS:SSD_HEADS + tn].astype(o_ref.dtype)


def _in_proj(x, g, w_t, q_scale):
    t = x.shape[0]
    tm, tn = PROJ_TM, PROJ_TN
    assert w_t.shape[0] == PROJ_MAIN_TILES * tn + SSD_HEADS + tn
    return pl.pallas_call(
        functools.partial(_in_proj_body, q_scale),
        grid=(t // tm, PROJ_MAIN_TILES + 1),
        in_specs=[
            pl.BlockSpec((tm, D_MODEL), lambda i, j: (i, 0)),
            pl.BlockSpec((1, D_MODEL), lambda i, j: (0, 0)),
            pl.BlockSpec((tn, D_MODEL), lambda i, j: (jnp.minimum(j, PROJ_MAIN_TILES - 1), 0)),
            pl.BlockSpec((pl.Element(SSD_HEADS + tn), pl.Element(D_MODEL)),
                         lambda i, j: (PROJ_MAIN_TILES * tn, 0), pipeline_mode=pl.Buffered(1)),
        ],
        out_specs=[
            pl.BlockSpec((tm, tn), lambda i, j: (i, j)),
            pl.BlockSpec((tm, LANES), lambda i, j: (i, 0)),
        ],
        out_shape=[
            jax.ShapeDtypeStruct((t, PROJ_WIDTH), BF16),
            jax.ShapeDtypeStruct((t, LANES), F32),
        ],
        scratch_shapes=[pltpu.VMEM((tm, D_MODEL), BF16)],
        compiler_params=pltpu.CompilerParams(
            dimension_semantics=("parallel", "arbitrary"), vmem_limit_bytes=VMEM_LIMIT),
        name="in_proj",
    )(x, g, w_t, w_t)


def _attn_body(lam_init, seq, q_ref, k_ref, v_ref, lam_ref, g_ref, o_ref,
               qq_ref, m_ref, acc_ref, vx_ref, mask_ref):
    tq = ATTN_TQ
    heads = range(ATTN_HPS)
    lamv = lam_ref[...]
    lam = (jnp.exp(jnp.sum(lamv[0:1] * lamv[1:2], axis=-1, keepdims=True))
           - jnp.exp(jnp.sum(lamv[2:3] * lamv[3:4], axis=-1, keepdims=True)) + lam_init)
    lane = lax.broadcasted_iota(jnp.int32, (1, DA_PAIR), 1)
    first_half = lane < DA_HEAD_DIM
    ones_col = jnp.broadcast_to(jnp.where(lane == 0, 1.0, 0.0), (seq, DA_PAIR)).astype(BF16)
    for hh in heads:
        vx_ref[hh, :, :DA_PAIR] = v_ref[:, hh * DA_PAIR:(hh + 1) * DA_PAIR]
        vx_ref[hh, :, DA_PAIR:] = ones_col

    row = lax.broadcasted_iota(jnp.int32, (2 * tq, tq), 0)
    col = lax.broadcasted_iota(jnp.int32, (2 * tq, tq), 1)
    mask_ref[...] = jnp.where(jnp.where(row >= tq, row - tq, row) >= col, 0.0, -jnp.inf)

    def kv_block(slot, k0, width, first):
        def slab(hh, c):
            s = _dot_nt(qq_ref[hh], k_ref[pl.ds(pl.multiple_of(k0 + c, ATTN_KSLAB), ATTN_KSLAB),
                                          hh * DA_PAIR:(hh + 1) * DA_PAIR])
            d = c - (width - tq)
            return s + mask_ref[:, d:d + ATTN_KSLAB] if first and d >= 0 else s

        scores = [jnp.concatenate([slab(hh, c) for c in range(0, width, ATTN_KSLAB)], axis=1)
                  for hh in heads]
        for hh, s in zip(heads, scores):
            m_new = jnp.broadcast_to(jnp.max(s, axis=-1, keepdims=True), (2 * tq, LANES))
            if not first:
                m_old = m_ref[slot, hh]
                m_new = jnp.maximum(m_old, m_new)
                alpha = jnp.exp2(m_old - m_new)
                alpha2 = jnp.concatenate([alpha, alpha], axis=1)
            p = jnp.concatenate(
                [jnp.exp2(s[:, c:c + LANES] - m_new).astype(BF16) for c in range(0, width, LANES)],
                axis=1)
            vb = vx_ref[hh, pl.ds(k0, width), :]
            for r in (0, tq):
                pv = _dot(p[r:r + tq], vb)
                if first:
                    acc_ref[slot, hh, r:r + tq, :] = pv
                else:
                    acc_ref[slot, hh, r:r + tq, :] = (
                        acc_ref[slot, hh, r:r + tq, :] * alpha2[r:r + tq] + pv)
            m_ref[slot, hh] = m_new

    def load_queries(r0):
        for hh in heads:
            q = q_ref[pl.ds(r0, tq), hh * DA_PAIR:(hh + 1) * DA_PAIR]
            qq_ref[hh, 0:tq, :] = jnp.where(first_half, q, jnp.zeros_like(q))
            qq_ref[hh, tq:2 * tq, :] = jnp.where(first_half, jnp.zeros_like(q), q)

    def finish(slot, r0):
        for hh in heads:
            acc = acc_ref[slot, hh]
            o = acc[:, :DA_PAIR] / acc[:, DA_PAIR:DA_PAIR + 1]
            res = o[:tq] - lam * o[tq:]
            out = _rms(res, g_ref[...], SUBLN_EPS) * (1.0 - lam_init)
            o_ref[pl.ds(r0, tq), hh * DA_PAIR:(hh + 1) * DA_PAIR] = out.astype(o_ref.dtype)

    def earlier_keys(slot, a):
        def double_step(jj, c):
            kv_block(slot, pl.multiple_of(jj * (2 * tq), 2 * tq), 2 * tq, False)
            return c

        lax.fori_loop(0, a, double_step, 0)

    def tile_pair(a, carry):
        r_even = pl.multiple_of(a * (2 * tq), 2 * tq)
        r_odd = pl.multiple_of(r_even + tq, tq)
        load_queries(r_even)
        kv_block(0, r_even, tq, True)
        earlier_keys(0, a)
        load_queries(r_odd)
        kv_block(1, r_even, 2 * tq, True)
        finish(0, r_even)
        earlier_keys(1, a)
        finish(1, r_odd)
        return carry

    lax.fori_loop(0, seq // (2 * tq), tile_pair, 0)


def _diff_attn(proj, lam_params, subln_g, lam_init, batch, seq):
    t = batch * seq
    tq = ATTN_TQ
    w = ATTN_HPS * DA_PAIR
    steps = DA_HEADS // ATTN_HPS
    return pl.pallas_call(
        functools.partial(_attn_body, lam_init, seq),
        grid=(batch, steps),
        in_specs=[
            pl.BlockSpec((seq, w), lambda b, h: (b, h)),
            pl.BlockSpec((seq, w), lambda b, h: (b, steps + h)),
            pl.BlockSpec((seq, w), lambda b, h: (b, 2 * steps + h)),
            pl.BlockSpec((4, DA_HEAD_DIM), lambda b, h: (0, 0)),
            pl.BlockSpec((1, DA_PAIR), lambda b, h: (0, 0)),
        ],
        out_specs=pl.BlockSpec((seq, w), lambda b, h: (b, h)),
        out_shape=jax.ShapeDtypeStruct((t, DA_V), BF16),
        scratch_shapes=[
            pltpu.VMEM((ATTN_HPS, 2 * tq, DA_PAIR), BF16),
            pltpu.VMEM((2, ATTN_HPS, 2 * tq, LANES), F32),
            pltpu.VMEM((2, ATTN_HPS, 2 * tq, 2 * DA_PAIR), F32),
            pltpu.VMEM((ATTN_HPS, seq, 2 * DA_PAIR), BF16),
            pltpu.VMEM((2 * tq, tq), F32),
        ],
        compiler_params=pltpu.CompilerParams(
            dimension_semantics=("parallel", "parallel"), vmem_limit_bytes=VMEM_LIMIT),
        name="diff_attn",
    )(proj, proj, proj, lam_params, subln_g)


def _ssd_body(*refs):
    nz, nx = SSD_INNER // PROJ_PIECE, SSD_CONV_DIM // PROJ_PIECE
    z_refs, xbc_refs = refs[:nz], refs[nz:nz + nx]
    rest = refs[nz + nx:]
    xe_ref, rt_ref = rest[-3], rest[-2]

    @pl.when(pl.program_id(1) == 0)
    def _():
        xe_ref[0:SSD_HIST, :] = jnp.zeros((SSD_HIST, SSD_CONV_DIM), BF16)
        rt_ref[...] = jnp.zeros(rt_ref.shape, F32)

    for sub in range(SSD_SUB):
        _ssd_chunk(pl.ds(sub * SSD_CHUNK, SSD_CHUNK), z_refs, xbc_refs, *rest)


def _ssd_chunk(rows, z_refs, xbc_refs, dt_ref, shift_ref, cw_ref, cb_ref, dtb_ref, alog_ref,
               dexp_ref, ng_ref, y_ref, xe_ref, rt_ref, ybuf_ref):
    lc = SSD_CHUNK
    hist = SSD_HIST

    xb = jnp.concatenate([r[rows, :] for r in xbc_refs], axis=1)
    xe_ref[hist:hist + lc, :] = xb
    xe = xe_ref[...]
    xe_ref[0:hist, :] = xb[lc - hist:lc, :]
    cw = cw_ref[...].astype(BF16)
    taps = jnp.concatenate([xe * cw[k:k + 1, :] for k in range(SSD_CONV)], axis=0)
    act = _silu(_dot(shift_ref[...], taps) + cb_ref[...])

    dt = jax.nn.softplus(dt_ref[rows, :] + dtb_ref[...])
    adt = -jnp.exp(alog_ref[...]) * dt
    ri = lax.broadcasted_iota(jnp.int32, (lc, lc), 0)
    ci = lax.broadcasted_iota(jnp.int32, (lc, lc), 1)
    lower = ri >= ci
    tril = jnp.where(lower, 1.0, 0.0)
    acs = jnp.dot(tril, adt, precision=lax.Precision.HIGHEST, preferred_element_type=F32)
    acs_t = acs.T
    dt_t = dt.T
    tot_t = acs_t[:, lc - 1:lc]
    w_t = jnp.exp(tot_t - acs_t) * dt_t
    etot_t = jnp.exp(tot_t)
    src_t = acs_t - jnp.log(dt_t)

    lane = lax.broadcasted_iota(jnp.int32, (lc, LANES), 1)
    left = lane < SSD_HEAD_DIM
    zero16 = jnp.zeros((lc, LANES), BF16)

    for g in range(SSD_GROUPS):
        b0 = SSD_INNER + g * SSD_STATE
        c0 = SSD_INNER + SSD_GROUPS * SSD_STATE + g * SSD_STATE
        cg = act[:, c0:c0 + SSD_STATE]
        bg_t = act[:, b0:b0 + SSD_STATE].T
        cbg = _dot(cg.astype(BF16), bg_t.astype(BF16))
        for pr in range(SSD_HEADS_PER_GROUP // 2):
            h0 = g * SSD_HEADS_PER_GROUP + 2 * pr
            x0 = h0 * SSD_HEAD_DIM
            xpair = act[:, x0:x0 + LANES]
            rtpair = rt_ref[:, x0:x0 + LANES]
            x16 = xpair.astype(BF16)
            r16 = rtpair.astype(BF16)
            xl, xr = jnp.where(left, x16, zero16), jnp.where(left, zero16, x16)
            rl, rr = jnp.where(left, r16, zero16), jnp.where(left, zero16, r16)
            mh, ech, wh = [], [], []
            for h in (h0, h0 + 1):
                bc = jnp.broadcast_to(acs[:, h:h + 1], (lc, lc))
                seg = jnp.where(lower, bc - src_t[h:h + 1, :], -jnp.inf)
                mh.append((cbg * jnp.exp(seg)).astype(BF16))
                ech.append((cg * jnp.exp(bc)).astype(BF16))
                wh.append((bg_t * w_t[h:h + 1, :]).astype(BF16))
            ypair = _dot(jnp.concatenate(mh + ech, axis=1),
                         jnp.concatenate([xl, xr, rl, rr], axis=0))
            spair = _dot(jnp.concatenate(wh, axis=1), jnp.concatenate([xl, xr], axis=0))
            e0 = jnp.broadcast_to(etot_t[h0:h0 + 1, :], (1, LANES))
            e1 = jnp.broadcast_to(etot_t[h0 + 1:h0 + 2, :], (1, LANES))
            rt_ref[:, x0:x0 + LANES] = rtpair * jnp.where(left[0:1], e0, e1) + spair
            yp = ypair + dexp_ref[:, x0:x0 + LANES] * xpair
            zc = x0 % PROJ_PIECE
            zpair = z_refs[x0 // PROJ_PIECE][rows, zc:zc + LANES].astype(F32)
            ybuf_ref[:, x0:x0 + LANES] = yp * _silu(zpair)

    for g in range(SSD_GROUPS):
        g0 = g * SSD_GROUP_W
        yg = ybuf_ref[:, g0:g0 + SSD_GROUP_W]
        y_ref[rows, g0:g0 + SSD_GROUP_W] = _rms(
            yg, ng_ref[:, g0:g0 + SSD_GROUP_W], SUBLN_EPS).astype(y_ref.dtype)


def _ssd(proj, dt_raw, conv_w, conv_b, dt_bias, a_log, d_exp, norm_g, batch, seq):
    t = batch * seq
    lc = SSD_CHUNK
    tr = SSD_SUB * lc
    nc = seq // tr
    row = lambda b, c: b * nc + c
    rows_e = SSD_HIST + lc
    col = jnp.arange(SSD_CONV * rows_e)
    src = SSD_HIST + jnp.arange(lc)[:, None] - (SSD_CONV - 1) + col[None, :] // rows_e
    shift = (col[None, :] % rows_e == src).astype(BF16)

    def pieces(off, width):
        return [pl.BlockSpec((tr, PROJ_PIECE), functools.partial(
            lambda b, c, blk: (row(b, c), blk), blk=(off + p0) // PROJ_PIECE))
            for p0 in range(0, width, PROJ_PIECE)]

    n_pieces = (SSD_INNER + SSD_CONV_DIM) // PROJ_PIECE
    return pl.pallas_call(
        _ssd_body,
        grid=(batch, nc),
        in_specs=pieces(PROJ_Z_OFF, SSD_INNER) + pieces(PROJ_XBC_OFF, SSD_CONV_DIM) + [
            pl.BlockSpec((tr, LANES), lambda b, c: (row(b, c), 0)),
            pl.BlockSpec((lc, SSD_CONV * rows_e), lambda b, c: (0, 0)),
            pl.BlockSpec((SSD_CONV, SSD_CONV_DIM), lambda b, c: (0, 0)),
            pl.BlockSpec((1, SSD_CONV_DIM), lambda b, c: (0, 0)),
            pl.BlockSpec((1, LANES), lambda b, c: (0, 0)),
            pl.BlockSpec((1, LANES), lambda b, c: (0, 0)),
            pl.BlockSpec((1, SSD_INNER), lambda b, c: (0, 0)),
            pl.BlockSpec((1, SSD_INNER), lambda b, c: (0, 0)),
        ],
        out_specs=pl.BlockSpec((tr, SSD_INNER), lambda b, c: (row(b, c), 0)),
        out_shape=jax.ShapeDtypeStruct((t, SSD_INNER), BF16),
        scratch_shapes=[
            pltpu.VMEM((SSD_HIST + lc, SSD_CONV_DIM), BF16),
            pltpu.VMEM((SSD_STATE, SSD_INNER), F32),
            pltpu.VMEM((lc, SSD_INNER), F32),
        ],
        compiler_params=pltpu.CompilerParams(
            dimension_semantics=("parallel", "arbitrary"), vmem_limit_bytes=VMEM_LIMIT),
        name="ssd",
    )(*([proj] * n_pieces), dt_raw, shift, conv_w, conv_b, dt_bias, a_log, d_exp, norm_g)


def _mix_xattn_body(ao_ref, ys_ref, gl_ref, bg_ref, x_ref, wa_ref, ws_ref, wm_ref, mpost_ref,
                    xpre_ref, wq_ref, kv_ref, wo_ref, xpost_ref, o_ref):
    attn_out = _dot(ao_ref[...], wa_ref[...])
    ssd_out = _dot(ys_ref[...], ws_ref[...])
    gates = jax.nn.sigmoid(gl_ref[...].astype(F32) + bg_ref[...])
    mixed = gates[:, :D_MODEL] * attn_out + gates[:, D_MODEL:] * ssd_out
    mixed = _dot(mixed.astype(BF16), wm_ref[...])
    x = x_ref[...] + _rms(mixed, mpost_ref[...], NORM_EPS)

    hq = _rms(x, xpre_ref[...], NORM_EPS).astype(BF16)
    qx = (_dot(hq, wq_ref[...]) * (XA_HEAD_DIM ** -0.5)).astype(BF16)
    heads = []
    for h in range(XA_HEADS):
        c0 = h * XA_HEAD_DIM
        s = _dot_nt(qx[:, c0:c0 + XA_HEAD_DIM], kv_ref[:, c0:c0 + XA_HEAD_DIM])
        e = jnp.exp(s - jnp.max(s, axis=-1, keepdims=True))
        denom = jnp.sum(e, axis=-1, keepdims=True)
        oh = _dot(e.astype(BF16), kv_ref[:, D_MODEL + c0:D_MODEL + c0 + XA_HEAD_DIM])
        heads.append((oh / denom).astype(BF16))
    xo = _dot(jnp.concatenate(heads, axis=1), wo_ref[...])
    o_ref[...] = x + _rms(xo, xpost_ref[...], NORM_EPS)


def _mix_xattn(attn_o, y_ssd, proj, b_gate, x, w_attn, w_ssd, w_mix, mix_post_g,
               xa_pre_g, w_q, kv, w_o, xa_post_g, seq):
    t = x.shape[0]
    tm = MIX_TM
    gw = N_BRANCH * D_MODEL
    per_batch = seq // tm
    row_tile = lambda w: pl.BlockSpec((tm, w), lambda i: (i, 0))
    return pl.pallas_call(
        _mix_xattn_body,
        grid=(t // tm,),
        in_specs=[
            row_tile(DA_V),
            row_tile(SSD_INNER),
            pl.BlockSpec((tm, gw), lambda i: (i, PROJ_GATE_OFF // gw)),
            _const_spec((1, gw)),
            row_tile(D_MODEL),
            _const_spec((DA_V, D_MODEL)),
            _const_spec((SSD_INNER, D_MODEL)),
            _const_spec((D_MODEL, D_MODEL)),
            _const_spec((1, D_MODEL)),
            _const_spec((1, D_MODEL)),
            _const_spec((D_MODEL, D_MODEL)),
            pl.BlockSpec((MEM_LEN, 2 * D_MODEL), lambda i: (i // per_batch, 0)),
            _const_spec((D_MODEL, D_MODEL)),
            _const_spec((1, D_MODEL)),
        ],
        out_specs=row_tile(D_MODEL),
        out_shape=jax.ShapeDtypeStruct((t, D_MODEL), F32),
        compiler_params=pltpu.CompilerParams(
            dimension_semantics=("parallel",), vmem_limit_bytes=VMEM_LIMIT),
        name="mix_xattn",
    )(attn_o, y_ssd, proj, b_gate, x, w_attn, w_ssd, w_mix, mix_post_g,
      xa_pre_g, w_q, kv, w_o, xa_post_g)


def _row(v):
    return v.reshape(1, -1)


def _pad_lanes(v):
    return jnp.pad(v, ((0, 0), (0, LANES - v.shape[1])))


def _layer(x, mem, layer_idx, batch, seq,
           ffn1_pre_g, ffn1_post_g, ffn1_w_gu, ffn1_w_down,
           mix_pre_g, mix_post_g, w_in, b_gate,
           da_lambda_q1, da_lambda_k1, da_lambda_q2, da_lambda_k2, da_subln_g,
           ssd_conv_w, ssd_conv_b, ssd_dt_bias, ssd_A_log, ssd_D, ssd_norm_g,
           w_branch_attn, w_branch_ssd, w_mix_out,
           xa_pre_g, xa_post_g, mem_norm_g, xa_w_q, xa_w_kv, xa_w_o,
           ffn2_pre_g, ffn2_post_g, ffn2_w_gu, ffn2_w_down):
    x = _ffn(x, _row(ffn1_pre_g), ffn1_w_gu.astype(BF16), ffn1_w_down.astype(BF16),
             _row(ffn1_post_g))

    q_scale = DA_HEAD_DIM ** -0.5 * math.log2(math.e)
    proj, dt_raw = _in_proj(x, _row(mix_pre_g), w_in.T.astype(BF16), q_scale)

    lam_init = 0.8 - 0.6 * math.exp(-0.3 * layer_idx)
    lam_params = jnp.stack([da_lambda_q1, da_lambda_k1, da_lambda_q2, da_lambda_k2])
    attn_o = _diff_attn(proj, lam_params, _row(da_subln_g), lam_init, batch, seq)

    y_ssd = _ssd(proj, dt_raw, ssd_conv_w, _row(ssd_conv_b), _pad_lanes(_row(ssd_dt_bias)),
                 _pad_lanes(_row(ssd_A_log)), _row(jnp.repeat(ssd_D, SSD_HEAD_DIM)),
                 _row(ssd_norm_g), batch, seq)

    mem2 = mem.reshape(batch * MEM_LEN, D_MODEL)
    kv = _norm_proj(mem2, _row(mem_norm_g), xa_w_kv.astype(BF16), MEM_LEN)
    x = _mix_xattn(attn_o, y_ssd, proj, _row(b_gate), x, w_branch_attn.astype(BF16),
                   w_branch_ssd.astype(BF16), w_mix_out.astype(BF16), _row(mix_post_g),
                   _row(xa_pre_g), xa_w_q.astype(BF16), kv, xa_w_o.astype(BF16),
                   _row(xa_post_g), seq)

    x = _ffn(x, _row(ffn2_pre_g), ffn2_w_gu.astype(BF16), ffn2_w_down.astype(BF16),
             _row(ffn2_post_g))
    return x


def kernel(x, mem, ffn1_pre_g, ffn1_post_g, ffn1_w_gu, ffn1_w_down, mix_pre_g, mix_post_g, w_in, b_gate, da_lambda_q1, da_lambda_k1, da_lambda_q2, da_lambda_k2, da_subln_g, ssd_conv_w, ssd_conv_b, ssd_dt_bias, ssd_A_log, ssd_D, ssd_norm_g, w_branch_attn, w_branch_ssd, w_mix_out, xa_pre_g, xa_post_g, mem_norm_g, xa_w_q, xa_w_kv, xa_w_o, ffn2_pre_g, ffn2_post_g, ffn2_w_gu, ffn2_w_down):
    batch, seq, d = x.shape
    params = (ffn1_pre_g, ffn1_post_g, ffn1_w_gu, ffn1_w_down, mix_pre_g, mix_post_g, w_in, b_gate,
              da_lambda_q1, da_lambda_k1, da_lambda_q2, da_lambda_k2, da_subln_g,
              ssd_conv_w, ssd_conv_b, ssd_dt_bias, ssd_A_log, ssd_D, ssd_norm_g,
              w_branch_attn, w_branch_ssd, w_mix_out,
              xa_pre_g, xa_post_g, mem_norm_g, xa_w_q, xa_w_kv, xa_w_o,
              ffn2_pre_g, ffn2_post_g, ffn2_w_gu, ffn2_w_down)
    h = x.reshape(batch * seq, d)
    for layer in range(ffn1_pre_g.shape[0]):
        h = _layer(h, mem, layer, batch, seq, *[p[layer] for p in params])
    return h.reshape(batch, seq, d)
```

```python
import functools
import math

import jax
import jax.numpy as jnp
from jax import lax
from jax.experimental import pallas as pl
from jax.experimental.pallas import tpu as pltpu

F32 = jnp.float32
BF16 = jnp.bfloat16

D_MODEL = 1024
MEM_LEN = 256
DA_HEADS = 8
DA_HEAD_DIM = 64
DA_PAIR = 2 * DA_HEAD_DIM
DA_QK = DA_HEADS * DA_PAIR
DA_V = DA_HEADS * DA_PAIR
SSD_INNER = 2 * D_MODEL
SSD_HEAD_DIM = 64
SSD_HEADS = SSD_INNER // SSD_HEAD_DIM
SSD_GROUPS = 4
SSD_HEADS_PER_GROUP = SSD_HEADS // SSD_GROUPS
SSD_STATE = 128
SSD_CONV = 4
SSD_CHUNK = 128
SSD_GROUP_W = SSD_INNER // SSD_GROUPS
SSD_SUB = 4
SSD_HIST = 16
SSD_CONV_DIM = SSD_INNER + 2 * SSD_GROUPS * SSD_STATE
XA_HEADS = 4
XA_HEAD_DIM = D_MODEL // XA_HEADS
D_FF = 2816
N_BRANCH = 2
NORM_EPS = 1e-6
SUBLN_EPS = 1e-5

LANES = 128
SUBLANES = 8
VMEM_LIMIT = 56 * 1024 * 1024

PROJ_Z_OFF = 3 * DA_QK
PROJ_XBC_OFF = PROJ_Z_OFF + SSD_INNER
PROJ_GATE_OFF = PROJ_XBC_OFF + SSD_CONV_DIM
PROJ_WIDTH = PROJ_GATE_OFF + N_BRANCH * D_MODEL
PROJ_PIECE = 1024

FFN_TM = 512
FFN_CHUNK = 256
PROJ_TM = 1024
PROJ_TN = 2048
PROJ_MAIN_TILES = PROJ_GATE_OFF // PROJ_TN
ATTN_TQ = 512
ATTN_KSLAB = 256
ATTN_HPS = 2
MIX_TM = 512


def _rms(x, g, eps):
    return x * lax.rsqrt(jnp.mean(x * x, axis=-1, keepdims=True) + eps) * g


def _silu(x):
    h = 0.5 * x
    return h + h * jnp.tanh(h)


def _dot(a, b):
    return jnp.dot(a, b, preferred_element_type=F32)


def _dot_nt(a, b):
    return lax.dot_general(a, b, (((1,), (1,)), ((), ())), preferred_element_type=F32)


def _const_spec(shape):
    nd = len(shape)
    return pl.BlockSpec(shape, lambda *_: (0,) * nd, pipeline_mode=pl.Buffered(1))


def _ffn_body(x_ref, pre_ref, wgu_ref, wd_ref, post_ref, o_ref):
    x = x_ref[...]
    hn = _rms(x, pre_ref[...], NORM_EPS)
    acc = None
    for c0 in range(0, D_FF, FFN_CHUNK):
        g = _dot(hn, wgu_ref[:, c0:c0 + FFN_CHUNK])
        u = _dot(hn, wgu_ref[:, D_FF + c0:D_FF + c0 + FFN_CHUNK])
        a = _silu(g) * u
        d = _dot(a, wd_ref[c0:c0 + FFN_CHUNK, :])
        acc = d if acc is None else acc + d
    o_ref[...] = x + 0.5 * _rms(acc, post_ref[...], NORM_EPS)


def _ffn(x, pre_g, w_gu, w_down, post_g):
    t = x.shape[0]
    return pl.pallas_call(
        _ffn_body,
        grid=(t // FFN_TM,),
        in_specs=[
            pl.BlockSpec((FFN_TM, D_MODEL), lambda i: (i, 0)),
            _const_spec((1, D_MODEL)),
            _const_spec((D_MODEL, 2 * D_FF)),
            _const_spec((D_FF, D_MODEL)),
            _const_spec((1, D_MODEL)),
        ],
        out_specs=pl.BlockSpec((FFN_TM, D_MODEL), lambda i: (i, 0)),
        out_shape=jax.ShapeDtypeStruct((t, D_MODEL), F32),
        compiler_params=pltpu.CompilerParams(
            dimension_semantics=("parallel",), vmem_limit_bytes=VMEM_LIMIT),
        name="ffn",
    )(x, pre_g, w_gu, w_down, post_g)


def _norm_proj_body(x_ref, g_ref, w_ref, o_ref):
    hn = _rms(x_ref[...], g_ref[...], NORM_EPS)
    o_ref[...] = _dot(hn, w_ref[...]).astype(o_ref.dtype)


def _norm_proj(x, g, w, tm):
    t, n = x.shape[0], w.shape[1]
    return pl.pallas_call(
        _norm_proj_body,
        grid=(t // tm,),
        in_specs=[
            pl.BlockSpec((tm, D_MODEL), lambda i: (i, 0)),
            _const_spec((1, D_MODEL)),
            _const_spec((D_MODEL, n)),
        ],
        out_specs=pl.BlockSpec((tm, n), lambda i: (i, 0)),
        out_shape=jax.ShapeDtypeStruct((t, n), BF16),
        compiler_params=pltpu.CompilerParams(
            dimension_semantics=("parallel",), vmem_limit_bytes=VMEM_LIMIT),
        name="norm_proj",
    )(x, g, w)


def _in_proj_body(q_scale, x_ref, g_ref, w_ref, wt_ref, o_ref, dt_ref, hn_ref):
    j = pl.program_id(1)
    tn = o_ref.shape[1]

    @pl.when(j == 0)
    def _():
        hn_ref[...] = _rms(x_ref[...], g_ref[...], NORM_EPS)

    @pl.when(j < PROJ_MAIN_TILES)
    def _():
        col = j * tn + lax.broadcasted_iota(jnp.int32, (1, tn), 1)
        scale = jnp.where(col < DA_QK, q_scale, 1.0)
        o_ref[...] = (_dot_nt(hn_ref[...], w_ref[...]) * scale).astype(o_ref.dtype)

    @pl.when(j == PROJ_MAIN_TILES)
    def _():
        tail = _dot_nt(hn_ref[...], wt_ref[...])
        dt_ref[...] = tail[:, :LANES]
        o_ref[...] = tail[:, SSD_HEADS:SSD_HEADS + tn].astype(o_ref.dtype)


def _in_proj(x, g, w_t, q_scale):
    t = x.shape[0]
    tm, tn = PROJ_TM, PROJ_TN
    assert w_t.shape[0] == PROJ_MAIN_TILES * tn + SSD_HEADS + tn
    return pl.pallas_call(
        functools.partial(_in_proj_body, q_scale),
        grid=(t // tm, PROJ_MAIN_TILES + 1),
        in_specs=[
            pl.BlockSpec((tm, D_MODEL), lambda i, j: (i, 0)),
            pl.BlockSpec((1, D_MODEL), lambda i, j: (0, 0)),
            pl.BlockSpec((tn, D_MODEL), lambda i, j: (jnp.minimum(j, PROJ_MAIN_TILES - 1), 0)),
            pl.BlockSpec((pl.Element(SSD_HEADS + tn), pl.Element(D_MODEL)),
                         lambda i, j: (PROJ_MAIN_TILES * tn, 0), pipeline_mode=pl.Buffered(1)),
        ],
        out_specs=[
            pl.BlockSpec((tm, tn), lambda i, j: (i, j)),
            pl.BlockSpec((tm, LANES), lambda i, j: (i, 0)),
        ],
        out_shape=[
            jax.ShapeDtypeStruct((t, PROJ_WIDTH), BF16),
            jax.ShapeDtypeStruct((t, LANES), F32),
        ],
        scratch_shapes=[pltpu.VMEM((tm, D_MODEL), F32)],
        compiler_params=pltpu.CompilerParams(
            dimension_semantics=("parallel", "arbitrary"), vmem_limit_bytes=VMEM_LIMIT),
        name="in_proj",
    )(x, g, w_t, w_t)


def _attn_body(lam_init, seq, q_ref, k_ref, v_ref, lam_ref, g_ref, o_ref,
               qq_ref, m_ref, acc_ref, vx_ref, mask_ref):
    tq = ATTN_TQ
    heads = range(ATTN_HPS)
    lamv = lam_ref[...]
    lam = (jnp.exp(jnp.sum(lamv[0:1] * lamv[1:2], axis=-1, keepdims=True))
           - jnp.exp(jnp.sum(lamv[2:3] * lamv[3:4], axis=-1, keepdims=True)) + lam_init)
    lane = lax.broadcasted_iota(jnp.int32, (1, DA_PAIR), 1)
    first_half = lane < DA_HEAD_DIM
    ones_col = jnp.broadcast_to(jnp.where(lane == 0, 1.0, 0.0), (seq, DA_PAIR)).astype(BF16)
    for hh in heads:
        vx_ref[hh, :, :DA_PAIR] = v_ref[:, hh * DA_PAIR:(hh + 1) * DA_PAIR]
        vx_ref[hh, :, DA_PAIR:] = ones_col

    row = lax.broadcasted_iota(jnp.int32, (2 * tq, tq), 0)
    col = lax.broadcasted_iota(jnp.int32, (2 * tq, tq), 1)
    mask_ref[...] = jnp.where(jnp.where(row >= tq, row - tq, row) >= col, 0.0, -jnp.inf)

    def kv_block(slot, k0, width, first):
        def slab(hh, c):
            s = _dot_nt(qq_ref[hh], k_ref[pl.ds(pl.multiple_of(k0 + c, ATTN_KSLAB), ATTN_KSLAB),
                                          hh * DA_PAIR:(hh + 1) * DA_PAIR])
            d = c - (width - tq)
            return s + mask_ref[:, d:d + ATTN_KSLAB] if first and d >= 0 else s

        scores = [jnp.concatenate([slab(hh, c) for c in range(0, width, ATTN_KSLAB)], axis=1)
                  for hh in heads]
        for hh, s in zip(heads, scores):
            m_new = jnp.broadcast_to(jnp.max(s, axis=-1, keepdims=True), (2 * tq, LANES))
            if not first:
                m_old = m_ref[slot, hh]
                m_new = jnp.maximum(m_old, m_new)
                alpha = jnp.exp2(m_old - m_new)
                alpha2 = jnp.concatenate([alpha, alpha], axis=1)
            p = jnp.concatenate(
                [jnp.exp2(s[:, c:c + LANES] - m_new).astype(BF16) for c in range(0, width, LANES)],
                axis=1)
            vb = vx_ref[hh, pl.ds(k0, width), :]
            for r in (0, tq):
                pv = _dot(p[r:r + tq], vb)
                if first:
                    acc_ref[slot, hh, r:r + tq, :] = pv
                else:
                    acc_ref[slot, hh, r:r + tq, :] = (
                        acc_ref[slot, hh, r:r + tq, :] * alpha2[r:r + tq] + pv)
            m_ref[slot, hh] = m_new

    def load_queries(r0):
        for hh in heads:
            q = q_ref[pl.ds(r0, tq), hh * DA_PAIR:(hh + 1) * DA_PAIR]
            qq_ref[hh, 0:tq, :] = jnp.where(first_half, q, jnp.zeros_like(q))
            qq_ref[hh, tq:2 * tq, :] = jnp.where(first_half, jnp.zeros_like(q), q)

    def finish(slot, r0):
        for hh in heads:
            acc = acc_ref[slot, hh]
            o = acc[:, :DA_PAIR] / acc[:, DA_PAIR:DA_PAIR + 1]
            res = o[:tq] - lam * o[tq:]
            out = _rms(res, g_ref[...], SUBLN_EPS) * (1.0 - lam_init)
            o_ref[pl.ds(r0, tq), hh * DA_PAIR:(hh + 1) * DA_PAIR] = out.astype(o_ref.dtype)

    def earlier_keys(slot, a):
        def double_step(jj, c):
            kv_block(slot, pl.multiple_of(jj * (2 * tq), 2 * tq), 2 * tq, False)
            return c

        lax.fori_loop(0, a, double_step, 0)

    def tile_pair(a, carry):
        r_even = pl.multiple_of(a * (2 * tq), 2 * tq)
        r_odd = pl.multiple_of(r_even + tq, tq)
        load_queries(r_even)
        kv_block(0, r_even, tq, True)
        earlier_keys(0, a)
        load_queries(r_odd)
        kv_block(1, r_even, 2 * tq, True)
        finish(0, r_even)
        earlier_keys(1, a)
        finish(1, r_odd)
        return carry

    lax.fori_loop(0, seq // (2 * tq), tile_pair, 0)


def _diff_attn(proj, lam_params, subln_g, lam_init, batch, seq):
    t = batch * seq
    tq = ATTN_TQ
    w = ATTN_HPS * DA_PAIR
    steps = DA_HEADS // ATTN_HPS
    return pl.pallas_call(
        functools.partial(_attn_body, lam_init, seq),
        grid=(batch, steps),
        in_specs=[
            pl.BlockSpec((seq, w), lambda b, h: (b, h)),
            pl.BlockSpec((seq, w), lambda b, h: (b, steps + h)),
            pl.BlockSpec((seq, w), lambda b, h: (b, 2 * steps + h)),
            pl.BlockSpec((4, DA_HEAD_DIM), lambda b, h: (0, 0)),
            pl.BlockSpec((1, DA_PAIR), lambda b, h: (0, 0)),
        ],
        out_specs=pl.BlockSpec((seq, w), lambda b, h: (b, h)),
        out_shape=jax.ShapeDtypeStruct((t, DA_V), BF16),
        scratch_shapes=[
            pltpu.VMEM((ATTN_HPS, 2 * tq, DA_PAIR), BF16),
            pltpu.VMEM((2, ATTN_HPS, 2 * tq, LANES), F32),
            pltpu.VMEM((2, ATTN_HPS, 2 * tq, 2 * DA_PAIR), F32),
            pltpu.VMEM((ATTN_HPS, seq, 2 * DA_PAIR), BF16),
            pltpu.VMEM((2 * tq, tq), F32),
        ],
        compiler_params=pltpu.CompilerParams(
            dimension_semantics=("parallel", "parallel"), vmem_limit_bytes=VMEM_LIMIT),
        name="diff_attn",
    )(proj, proj, proj, lam_params, subln_g)


def _ssd_body(*refs):
    nz, nx = SSD_INNER // PROJ_PIECE, SSD_CONV_DIM // PROJ_PIECE
    z_refs, xbc_refs = refs[:nz], refs[nz:nz + nx]
    rest = refs[nz + nx:]
    xe_ref, rt_ref = rest[-3], rest[-2]

    @pl.when(pl.program_id(1) == 0)
    def _():
        xe_ref[0:SSD_HIST, :] = jnp.zeros((SSD_HIST, SSD_CONV_DIM), BF16)
        rt_ref[...] = jnp.zeros(rt_ref.shape, F32)

    for sub in range(SSD_SUB):
        _ssd_chunk(pl.ds(sub * SSD_CHUNK, SSD_CHUNK), z_refs, xbc_refs, *rest)


def _ssd_chunk(rows, z_refs, xbc_refs, dt_ref, shift_ref, cw_ref, cb_ref, dtb_ref, alog_ref,
               dexp_ref, ng_ref, y_ref, xe_ref, rt_ref, ybuf_ref):
    lc = SSD_CHUNK
    hist = SSD_HIST

    xb = jnp.concatenate([r[rows, :] for r in xbc_refs], axis=1)
    xe_ref[hist:hist + lc, :] = xb
    xe = xe_ref[...]
    xe_ref[0:hist, :] = xb[lc - hist:lc, :]
    cw = cw_ref[...].astype(BF16)
    taps = jnp.concatenate([xe * cw[k:k + 1, :] for k in range(SSD_CONV)], axis=0)
    act = _silu(_dot(shift_ref[...], taps) + cb_ref[...])

    dt = jax.nn.softplus(dt_ref[rows, :] + dtb_ref[...])
    adt = -jnp.exp(alog_ref[...]) * dt
    ri = lax.broadcasted_iota(jnp.int32, (lc, lc), 0)
    ci = lax.broadcasted_iota(jnp.int32, (lc, lc), 1)
    lower = ri >= ci
    tril = jnp.where(lower, 1.0, 0.0)
    acs = jnp.dot(tril, adt, precision=lax.Precision.HIGHEST, preferred_element_type=F32)
    acs_t = acs.T
    dt_t = dt.T
    tot_t = acs_t[:, lc - 1:lc]
    w_t = jnp.exp(tot_t - acs_t) * dt_t
    etot_t = jnp.exp(tot_t)
    src_t = acs_t - jnp.log(dt_t)

    lane = lax.broadcasted_iota(jnp.int32, (lc, LANES), 1)
    left = lane < SSD_HEAD_DIM
    zero16 = jnp.zeros((lc, LANES), BF16)

    for g in range(SSD_GROUPS):
        b0 = SSD_INNER + g * SSD_STATE
        c0 = SSD_INNER + SSD_GROUPS * SSD_STATE + g * SSD_STATE
        cg = act[:, c0:c0 + SSD_STATE]
        bg_t = act[:, b0:b0 + SSD_STATE].T
        cbg = _dot(cg.astype(BF16), bg_t.astype(BF16))
        for pr in range(SSD_HEADS_PER_GROUP // 2):
            h0 = g * SSD_HEADS_PER_GROUP + 2 * pr
            x0 = h0 * SSD_HEAD_DIM
            xpair = act[:, x0:x0 + LANES]
            rtpair = rt_ref[:, x0:x0 + LANES]
            x16 = xpair.astype(BF16)
            r16 = rtpair.astype(BF16)
            xl, xr = jnp.where(left, x16, zero16), jnp.where(left, zero16, x16)
            rl, rr = jnp.where(left, r16, zero16), jnp.where(left, zero16, r16)
            mh, ech, wh = [], [], []
            for h in (h0, h0 + 1):
                bc = jnp.broadcast_to(acs[:, h:h + 1], (lc, lc))
                seg = jnp.where(lower, bc - src_t[h:h + 1, :], -jnp.inf)
                mh.append((cbg * jnp.exp(seg)).astype(BF16))
                ech.append((cg * jnp.exp(bc)).astype(BF16))
                wh.append((bg_t * w_t[h:h + 1, :]).astype(BF16))
            ypair = _dot(jnp.concatenate(mh + ech, axis=1),
                         jnp.concatenate([xl, xr, rl, rr], axis=0))
            spair = _dot(jnp.concatenate(wh, axis=1), jnp.concatenate([xl, xr], axis=0))
            e0 = jnp.broadcast_to(etot_t[h0:h0 + 1, :], (1, LANES))
            e1 = jnp.broadcast_to(etot_t[h0 + 1:h0 + 2, :], (1, LANES))
            rt_ref[:, x0:x0 + LANES] = rtpair * jnp.where(left[0:1], e0, e1) + spair
            yp = ypair + dexp_ref[:, x0:x0 + LANES] * xpair
            zc = x0 % PROJ_PIECE
            zpair = z_refs[x0 // PROJ_PIECE][rows, zc:zc + LANES].astype(F32)
            ybuf_ref[:, x0:x0 + LANES] = yp * _silu(zpair)

    for g in range(SSD_GROUPS):
        g0 = g * SSD_GROUP_W
        yg = ybuf_ref[:, g0:g0 + SSD_GROUP_W]
        y_ref[rows, g0:g0 + SSD_GROUP_W] = _rms(
            yg, ng_ref[:, g0:g0 + SSD_GROUP_W], SUBLN_EPS).astype(y_ref.dtype)


def _ssd(proj, dt_raw, conv_w, conv_b, dt_bias, a_log, d_exp, norm_g, batch, seq):
    t = batch * seq
    lc = SSD_CHUNK
    tr = SSD_SUB * lc
    nc = seq // tr
    row = lambda b, c: b * nc + c
    rows_e = SSD_HIST + lc
    col = jnp.arange(SSD_CONV * rows_e)
    src = SSD_HIST + jnp.arange(lc)[:, None] - (SSD_CONV - 1) + col[None, :] // rows_e
    shift = (col[None, :] % rows_e == src).astype(BF16)

    def pieces(off, width):
        return [pl.BlockSpec((tr, PROJ_PIECE), functools.partial(
            lambda b, c, blk: (row(b, c), blk), blk=(off + p0) // PROJ_PIECE))
            for p0 in range(0, width, PROJ_PIECE)]

    n_pieces = (SSD_INNER + SSD_CONV_DIM) // PROJ_PIECE
    return pl.pallas_call(
        _ssd_body,
        grid=(batch, nc),
        in_specs=pieces(PROJ_Z_OFF, SSD_INNER) + pieces(PROJ_XBC_OFF, SSD_CONV_DIM) + [
            pl.BlockSpec((tr, LANES), lambda b, c: (row(b, c), 0)),
            pl.BlockSpec((lc, SSD_CONV * rows_e), lambda b, c: (0, 0)),
            pl.BlockSpec((SSD_CONV, SSD_CONV_DIM), lambda b, c: (0, 0)),
            pl.BlockSpec((1, SSD_CONV_DIM), lambda b, c: (0, 0)),
            pl.BlockSpec((1, LANES), lambda b, c: (0, 0)),
            pl.BlockSpec((1, LANES), lambda b, c: (0, 0)),
            pl.BlockSpec((1, SSD_INNER), lambda b, c: (0, 0)),
            pl.BlockSpec((1, SSD_INNER), lambda b, c: (0, 0)),
        ],
        out_specs=pl.BlockSpec((tr, SSD_INNER), lambda b, c: (row(b, c), 0)),
        out_shape=jax.ShapeDtypeStruct((t, SSD_INNER), BF16),
        scratch_shapes=[
            pltpu.VMEM((SSD_HIST + lc, SSD_CONV_DIM), BF16),
            pltpu.VMEM((SSD_STATE, SSD_INNER), F32),
            pltpu.VMEM((lc, SSD_INNER), F32),
        ],
        compiler_params=pltpu.CompilerParams(
            dimension_semantics=("parallel", "arbitrary"), vmem_limit_bytes=VMEM_LIMIT),
        name="ssd",
    )(*([proj] * n_pieces), dt_raw, shift, conv_w, conv_b, dt_bias, a_log, d_exp, norm_g)


def _mix_xattn_body(ao_ref, ys_ref, gl_ref, bg_ref, x_ref, wa_ref, ws_ref, wm_ref, mpost_ref,
                    xpre_ref, wq_ref, kv_ref, wo_ref, xpost_ref, o_ref):
    attn_out = _dot(ao_ref[...], wa_ref[...])
    ssd_out = _dot(ys_ref[...], ws_ref[...])
    gates = jax.nn.sigmoid(gl_ref[...].astype(F32) + bg_ref[...])
    mixed = gates[:, :D_MODEL] * attn_out + gates[:, D_MODEL:] * ssd_out
    mixed = _dot(mixed, wm_ref[...])
    x = x_ref[...] + _rms(mixed, mpost_ref[...], NORM_EPS)

    hq = _rms(x, xpre_ref[...], NORM_EPS)
    qx = (_dot(hq, wq_ref[...]) * (XA_HEAD_DIM ** -0.5)).astype(BF16)
    heads = []
    for h in range(XA_HEADS):
        c0 = h * XA_HEAD_DIM
        s = _dot_nt(qx[:, c0:c0 + XA_HEAD_DIM], kv_ref[:, c0:c0 + XA_HEAD_DIM])
        e = jnp.exp(s - jnp.max(s, axis=-1, keepdims=True))
        denom = jnp.sum(e, axis=-1, keepdims=True)
        oh = _dot(e.astype(BF16), kv_ref[:, D_MODEL + c0:D_MODEL + c0 + XA_HEAD_DIM])
        heads.append(oh / denom)
    xo = _dot(jnp.concatenate(heads, axis=1), wo_ref[...])
    o_ref[...] = x + _rms(xo, xpost_ref[...], NORM_EPS)


def _mix_xattn(attn_o, y_ssd, proj, b_gate, x, w_attn, w_ssd, w_mix, mix_post_g,
               xa_pre_g, w_q, kv, w_o, xa_post_g, seq):
    t = x.shape[0]
    tm = MIX_TM
    gw = N_BRANCH * D_MODEL
    per_batch = seq // tm
    row_tile = lambda w: pl.BlockSpec((tm, w), lambda i: (i, 0))
    return pl.pallas_call(
        _mix_xattn_body,
        grid=(t // tm,),
        in_specs=[
            row_tile(DA_V),
            row_tile(SSD_INNER),
            pl.BlockSpec((tm, gw), lambda i: (i, PROJ_GATE_OFF // gw)),
            _const_spec((1, gw)),
            row_tile(D_MODEL),
            _const_spec((DA_V, D_MODEL)),
            _const_spec((SSD_INNER, D_MODEL)),
            _const_spec((D_MODEL, D_MODEL)),
            _const_spec((1, D_MODEL)),
            _const_spec((1, D_MODEL)),
            _const_spec((D_MODEL, D_MODEL)),
            pl.BlockSpec((MEM_LEN, 2 * D_MODEL), lambda i: (i // per_batch, 0)),
            _const_spec((D_MODEL, D_MODEL)),
            _const_spec((1, D_MODEL)),
        ],
        out_specs=row_tile(D_MODEL),
        out_shape=jax.ShapeDtypeStruct((t, D_MODEL), F32),
        compiler_params=pltpu.CompilerParams(
            dimension_semantics=("parallel",), vmem_limit_bytes=VMEM_LIMIT),
        name="mix_xattn",
    )(attn_o, y_ssd, proj, b_gate, x, w_attn, w_ssd, w_mix, mix_post_g,
      xa_pre_g, w_q, kv, w_o, xa_post_g)


def _row(v):
    return v.reshape(1, -1)


def _pad_lanes(v):
    return jnp.pad(v, ((0, 0), (0, LANES - v.shape[1])))


def _layer(x, mem, layer_idx, batch, seq,
           ffn1_pre_g, ffn1_post_g, ffn1_w_gu, ffn1_w_down,
           mix_pre_g, mix_post_g, w_in, b_gate,
           da_lambda_q1, da_lambda_k1, da_lambda_q2, da_lambda_k2, da_subln_g,
           ssd_conv_w, ssd_conv_b, ssd_dt_bias, ssd_A_log, ssd_D, ssd_norm_g,
           w_branch_attn, w_branch_ssd, w_mix_out,
           xa_pre_g, xa_post_g, mem_norm_g, xa_w_q, xa_w_kv, xa_w_o,
           ffn2_pre_g, ffn2_post_g, ffn2_w_gu, ffn2_w_down):
    x = _ffn(x, _row(ffn1_pre_g), ffn1_w_gu, ffn1_w_down, _row(ffn1_post_g))

    q_scale = DA_HEAD_DIM ** -0.5 * math.log2(math.e)
    proj, dt_raw = _in_proj(x, _row(mix_pre_g), w_in.T, q_scale)

    lam_init = 0.8 - 0.6 * math.exp(-0.3 * layer_idx)
    lam_params = jnp.stack([da_lambda_q1, da_lambda_k1, da_lambda_q2, da_lambda_k2])
    attn_o = _diff_attn(proj, lam_params, _row(da_subln_g), lam_init, batch, seq)

    y_ssd = _ssd(proj, dt_raw, ssd_conv_w, _row(ssd_conv_b), _pad_lanes(_row(ssd_dt_bias)),
                 _pad_lanes(_row(ssd_A_log)), _row(jnp.repeat(ssd_D, SSD_HEAD_DIM)),
                 _row(ssd_norm_g), batch, seq)

    mem2 = mem.reshape(batch * MEM_LEN, D_MODEL)
    kv = _norm_proj(mem2, _row(mem_norm_g), xa_w_kv, MEM_LEN)
    x = _mix_xattn(attn_o, y_ssd, proj, _row(b_gate), x, w_branch_attn, w_branch_ssd, w_mix_out,
                   _row(mix_post_g), _row(xa_pre_g), xa_w_q, kv, xa_w_o, _row(xa_post_g), seq)

    x = _ffn(x, _row(ffn2_pre_g), ffn2_w_gu, ffn2_w_down, _row(ffn2_post_g))
    return x


def kernel(x, mem, ffn1_pre_g, ffn1_post_g, ffn1_w_gu, ffn1_w_down, mix_pre_g, mix_post_g, w_in, b_gate, da_lambda_q1, da_lambda_k1, da_lambda_q2, da_lambda_k2, da_subln_g, ssd_conv_w, ssd_conv_b, ssd_dt_bias, ssd_A_log, ssd_D, ssd_norm_g, w_branch_attn, w_branch_ssd, w_mix_out, xa_pre_g, xa_post_g, mem_norm_g, xa_w_q, xa_w_kv, xa_w_o, ffn2_pre_g, ffn2_post_g, ffn2_w_gu, ffn2_w_down):
    batch, seq, d = x.shape
    params = (ffn1_pre_g, ffn1_post_g, ffn1_w_gu, ffn1_w_down, mix_pre_g, mix_post_g, w_in, b_gate,
              da_lambda_q1, da_lambda_k1, da_lambda_q2, da_lambda_k2, da_subln_g,
              ssd_conv_w, ssd_conv_b, ssd_dt_bias, ssd_A_log, ssd_D, ssd_norm_g,
              w_branch_attn, w_branch_ssd, w_mix_out,
              xa_pre_g, xa_post_g, mem_norm_g, xa_w_q, xa_w_kv, xa_w_o,
              ffn2_pre_g, ffn2_post_g, ffn2_w_gu, ffn2_w_down)
    h = x.reshape(batch * seq, d)
    for layer in range(ffn1_pre_g.shape[0]):
        h = _layer(h, mem, layer, batch, seq, *[p[layer] for p in params])
    return h.reshape(batch, seq, d)
```

```python
import functools
import math

import jax
import jax.numpy as jnp
from jax import lax
from jax.experimental import pallas as pl
from jax.experimental.pallas import tpu as pltpu

F32 = jnp.float32
BF16 = jnp.bfloat16

D_MODEL = 1024
MEM_LEN = 256
DA_HEADS = 8
DA_HEAD_DIM = 64
DA_PAIR = 2 * DA_HEAD_DIM
DA_QK = DA_HEADS * DA_PAIR
DA_V = DA_HEADS * DA_PAIR
SSD_INNER = 2 * D_MODEL
SSD_HEAD_DIM = 64
SSD_HEADS = SSD_INNER // SSD_HEAD_DIM
SSD_GROUPS = 4
SSD_HEADS_PER_GROUP = SSD_HEADS // SSD_GROUPS
SSD_STATE = 128
SSD_CONV = 4
SSD_CHUNK = 128
SSD_GROUP_W = SSD_INNER // SSD_GROUPS
SSD_SUB = 4
SSD_HIST = 16
SSD_CONV_DIM = SSD_INNER + 2 * SSD_GROUPS * SSD_STATE
XA_HEADS = 4
XA_HEAD_DIM = D_MODEL // XA_HEADS
D_FF = 2816
N_BRANCH = 2
NORM_EPS = 1e-6
SUBLN_EPS = 1e-5

LANES = 128
SUBLANES = 8
VMEM_LIMIT = 56 * 1024 * 1024

PROJ_Z_OFF = 3 * DA_QK
PROJ_XBC_OFF = PROJ_Z_OFF + SSD_INNER
PROJ_GATE_OFF = PROJ_XBC_OFF + SSD_CONV_DIM
PROJ_WIDTH = PROJ_GATE_OFF + N_BRANCH * D_MODEL
PROJ_PIECE = 1024

FFN_TM = 512
FFN_CHUNK = 256
PROJ_TM = 1024
PROJ_TN = 2048
PROJ_MAIN_TILES = PROJ_GATE_OFF // PROJ_TN
ATTN_TQ = 512
ATTN_KSLAB = 256
ATTN_HPS = 2
MIX_TM = 512


def _rms(x, g, eps):
    return x * lax.rsqrt(jnp.mean(x * x, axis=-1, keepdims=True) + eps) * g


def _silu(x):
    h = 0.5 * x
    return h + h * jnp.tanh(h)


def _dot(a, b):
    return jnp.dot(a, b, preferred_element_type=F32)


def _dot_nt(a, b):
    return lax.dot_general(a, b, (((1,), (1,)), ((), ())), preferred_element_type=F32)


def _const_spec(shape):
    nd = len(shape)
    return pl.BlockSpec(shape, lambda *_: (0,) * nd, pipeline_mode=pl.Buffered(1))


def _ffn_body(x_ref, pre_ref, wgu_ref, wd_ref, post_ref, o_ref):
    x = x_ref[...]
    hn = _rms(x, pre_ref[...], NORM_EPS)
    acc = None
    for c0 in range(0, D_FF, FFN_CHUNK):
        g = _dot(hn, wgu_ref[:, c0:c0 + FFN_CHUNK])
        u = _dot(hn, wgu_ref[:, D_FF + c0:D_FF + c0 + FFN_CHUNK])
        a = _silu(g) * u
        d = _dot(a, wd_ref[c0:c0 + FFN_CHUNK, :])
        acc = d if acc is None else acc + d
    o_ref[...] = x + 0.5 * _rms(acc, post_ref[...], NORM_EPS)


def _ffn(x, pre_g, w_gu, w_down, post_g):
    t = x.shape[0]
    return pl.pallas_call(
        _ffn_body,
        grid=(t // FFN_TM,),
        in_specs=[
            pl.BlockSpec((FFN_TM, D_MODEL), lambda i: (i, 0)),
            _const_spec((1, D_MODEL)),
            _const_spec((D_MODEL, 2 * D_FF)),
            _const_spec((D_FF, D_MODEL)),
            _const_spec((1, D_MODEL)),
        ],
        out_specs=pl.BlockSpec((FFN_TM, D_MODEL), lambda i: (i, 0)),
        out_shape=jax.ShapeDtypeStruct((t, D_MODEL), F32),
        compiler_params=pltpu.CompilerParams(
            dimension_semantics=("parallel",), vmem_limit_bytes=VMEM_LIMIT),
        name="ffn",
    )(x, pre_g, w_gu, w_down, post_g)


def _norm_proj_body(x_ref, g_ref, w_ref, o_ref):
    hn = _rms(x_ref[...], g_ref[...], NORM_EPS)
    o_ref[...] = _dot(hn, w_ref[...]).astype(o_ref.dtype)


def _norm_proj(x, g, w, tm):
    t, n = x.shape[0], w.shape[1]
    return pl.pallas_call(
        _norm_proj_body,
        grid=(t // tm,),
        in_specs=[
            pl.BlockSpec((tm, D_MODEL), lambda i: (i, 0)),
            _const_spec((1, D_MODEL)),
            _const_spec((D_MODEL, n)),
        ],
        out_specs=pl.BlockSpec((tm, n), lambda i: (i, 0)),
        out_shape=jax.ShapeDtypeStruct((t, n), BF16),
        compiler_params=pltpu.CompilerParams(
            dimension_semantics=("parallel",), vmem_limit_bytes=VMEM_LIMIT),
        name="norm_proj",
    )(x, g, w)


def _in_proj_body(q_scale, x_ref, g_ref, w_ref, wt_ref, o_ref, dt_ref):
    j = pl.program_id(0)
    tn = o_ref.shape[1]

    @pl.when(j < PROJ_MAIN_TILES)
    def _():
        hn = _rms(x_ref[...], g_ref[...], NORM_EPS)
        col = j * tn + lax.broadcasted_iota(jnp.int32, (1, tn), 1)
        scale = jnp.where(col < DA_QK, q_scale, 1.0)
        o_ref[...] = (_dot_nt(hn, w_ref[...]) * scale).astype(o_ref.dtype)

    @pl.when(j == PROJ_MAIN_TILES)
    def _():
        hn = _rms(x_ref[...], g_ref[...], NORM_EPS)
        tail = _dot_nt(hn, wt_ref[...])
        dt_ref[...] = tail[:, :LANES]
        o_ref[...] = tail[:, SSD_HEADS:SSD_HEADS + tn].astype(o_ref.dtype)


def _in_proj(x, g, w_t, q_scale):
    t = x.shape[0]
    tm, tn = PROJ_TM, PROJ_TN
    assert w_t.shape[0] == PROJ_MAIN_TILES * tn + SSD_HEADS + tn
    return pl.pallas_call(
        functools.partial(_in_proj_body, q_scale),
        grid=(PROJ_MAIN_TILES + 1, t // tm),
        in_specs=[
            pl.BlockSpec((tm, D_MODEL), lambda j, i: (i, 0)),
            pl.BlockSpec((1, D_MODEL), lambda j, i: (0, 0)),
            pl.BlockSpec((tn, D_MODEL), lambda j, i: (jnp.minimum(j, PROJ_MAIN_TILES - 1), 0)),
            pl.BlockSpec((pl.Element(SSD_HEADS + tn), pl.Element(D_MODEL)),
                         lambda j, i: (PROJ_MAIN_TILES * tn, 0), pipeline_mode=pl.Buffered(1)),
        ],
        out_specs=[
            pl.BlockSpec((tm, tn), lambda j, i: (i, j)),
            pl.BlockSpec((tm, LANES), lambda j, i: (jnp.where(j == PROJ_MAIN_TILES, i, 0), 0)),
        ],
        out_shape=[
            jax.ShapeDtypeStruct((t, PROJ_WIDTH), BF16),
            jax.ShapeDtypeStruct((t, LANES), F32),
        ],
        compiler_params=pltpu.CompilerParams(
            dimension_semantics=("arbitrary", "arbitrary"), vmem_limit_bytes=VMEM_LIMIT),
        name="in_proj",
    )(x, g, w_t, w_t)


def _attn_body(lam_init, seq, q_ref, k_ref, v_ref, lam_ref, g_ref, o_ref,
               qq_ref, m_ref, acc_ref, vx_ref, mask_ref):
    tq = ATTN_TQ
    heads = range(ATTN_HPS)
    lamv = lam_ref[...]
    lam = (jnp.exp(jnp.sum(lamv[0:1] * lamv[1:2], axis=-1, keepdims=True))
           - jnp.exp(jnp.sum(lamv[2:3] * lamv[3:4], axis=-1, keepdims=True)) + lam_init)
    lane = lax.broadcasted_iota(jnp.int32, (1, DA_PAIR), 1)
    first_half = lane < DA_HEAD_DIM
    ones_col = jnp.broadcast_to(jnp.where(lane == 0, 1.0, 0.0), (seq, DA_PAIR)).astype(BF16)
    for hh in heads:
        vx_ref[hh, :, :DA_PAIR] = v_ref[:, hh * DA_PAIR:(hh + 1) * DA_PAIR]
        vx_ref[hh, :, DA_PAIR:] = ones_col

    row = lax.broadcasted_iota(jnp.int32, (2 * tq, tq), 0)
    col = lax.broadcasted_iota(jnp.int32, (2 * tq, tq), 1)
    mask_ref[...] = jnp.where(jnp.where(row >= tq, row - tq, row) >= col, 0.0, -jnp.inf)

    def kv_block(slot, k0, width, first):
        def slab(hh, c):
            s = _dot_nt(qq_ref[hh], k_ref[pl.ds(pl.multiple_of(k0 + c, ATTN_KSLAB), ATTN_KSLAB),
                                          hh * DA_PAIR:(hh + 1) * DA_PAIR])
            d = c - (width - tq)
            return s + mask_ref[:, d:d + ATTN_KSLAB] if first and d >= 0 else s

        scores = [jnp.concatenate([slab(hh, c) for c in range(0, width, ATTN_KSLAB)], axis=1)
                  for hh in heads]
        for hh, s in zip(heads, scores):
            m_new = jnp.broadcast_to(jnp.max(s, axis=-1, keepdims=True), (2 * tq, LANES))
            if not first:
                m_old = m_ref[slot, hh]
                m_new = jnp.maximum(m_old, m_new)
                alpha = jnp.exp2(m_old - m_new)
                alpha2 = jnp.concatenate([alpha, alpha], axis=1)
            p = jnp.concatenate(
                [jnp.exp2(s[:, c:c + LANES] - m_new).astype(BF16) for c in range(0, width, LANES)],
                axis=1)
            vb = vx_ref[hh, pl.ds(k0, width), :]
            for r in (0, tq):
                pv = _dot(p[r:r + tq], vb)
                if first:
                    acc_ref[slot, hh, r:r + tq, :] = pv
                else:
                    acc_ref[slot, hh, r:r + tq, :] = (
                        acc_ref[slot, hh, r:r + tq, :] * alpha2[r:r + tq] + pv)
            m_ref[slot, hh] = m_new

    def load_queries(r0):
        for hh in heads:
            q = q_ref[pl.ds(r0, tq), hh * DA_PAIR:(hh + 1) * DA_PAIR]
            qq_ref[hh, 0:tq, :] = jnp.where(first_half, q, jnp.zeros_like(q))
            qq_ref[hh, tq:2 * tq, :] = jnp.where(first_half, jnp.zeros_like(q), q)

    def finish(slot, r0):
        for hh in heads:
            acc = acc_ref[slot, hh]
            o = acc[:, :DA_PAIR] / acc[:, DA_PAIR:DA_PAIR + 1]
            res = o[:tq] - lam * o[tq:]
            out = _rms(res, g_ref[...], SUBLN_EPS) * (1.0 - lam_init)
            o_ref[pl.ds(r0, tq), hh * DA_PAIR:(hh + 1) * DA_PAIR] = out.astype(o_ref.dtype)

    def earlier_keys(slot, a):
        def double_step(jj, c):
            kv_block(slot, pl.multiple_of(jj * (2 * tq), 2 * tq), 2 * tq, False)
            return c

        lax.fori_loop(0, a, double_step, 0)

    def tile_pair(a, carry):
        r_even = pl.multiple_of(a * (2 * tq), 2 * tq)
        r_odd = pl.multiple_of(r_even + tq, tq)
        load_queries(r_even)
        kv_block(0, r_even, tq, True)
        earlier_keys(0, a)
        load_queries(r_odd)
        kv_block(1, r_even, 2 * tq, True)
        finish(0, r_even)
        earlier_keys(1, a)
        finish(1, r_odd)
        return carry

    lax.fori_loop(0, seq // (2 * tq), tile_pair, 0)


def _diff_attn(proj, lam_params, subln_g, lam_init, batch, seq):
    t = batch * seq
    tq = ATTN_TQ
    w = ATTN_HPS * DA_PAIR
    steps = DA_HEADS // ATTN_HPS
    return pl.pallas_call(
        functools.partial(_attn_body, lam_init, seq),
        grid=(batch, steps),
        in_specs=[
            pl.BlockSpec((seq, w), lambda b, h: (b, h)),
            pl.BlockSpec((seq, w), lambda b, h: (b, steps + h)),
            pl.BlockSpec((seq, w), lambda b, h: (b, 2 * steps + h)),
            pl.BlockSpec((4, DA_HEAD_DIM), lambda b, h: (0, 0)),
            pl.BlockSpec((1, DA_PAIR), lambda b, h: (0, 0)),
        ],
        out_specs=pl.BlockSpec((seq, w), lambda b, h: (b, h)),
        out_shape=jax.ShapeDtypeStruct((t, DA_V), BF16),
        scratch_shapes=[
            pltpu.VMEM((ATTN_HPS, 2 * tq, DA_PAIR), BF16),
            pltpu.VMEM((2, ATTN_HPS, 2 * tq, LANES), F32),
            pltpu.VMEM((2, ATTN_HPS, 2 * tq, 2 * DA_PAIR), F32),
            pltpu.VMEM((ATTN_HPS, seq, 2 * DA_PAIR), BF16),
            pltpu.VMEM((2 * tq, tq), F32),
        ],
        compiler_params=pltpu.CompilerParams(
            dimension_semantics=("parallel", "parallel"), vmem_limit_bytes=VMEM_LIMIT),
        name="diff_attn",
    )(proj, proj, proj, lam_params, subln_g)


def _ssd_body(*refs):
    nz, nx = SSD_INNER // PROJ_PIECE, SSD_CONV_DIM // PROJ_PIECE
    z_refs, xbc_refs = refs[:nz], refs[nz:nz + nx]
    rest = refs[nz + nx:]
    xe_ref, rt_ref = rest[-3], rest[-2]

    @pl.when(pl.program_id(1) == 0)
    def _():
        xe_ref[0:SSD_HIST, :] = jnp.zeros((SSD_HIST, SSD_CONV_DIM), BF16)
        rt_ref[...] = jnp.zeros(rt_ref.shape, F32)

    for sub in range(SSD_SUB):
        _ssd_chunk(pl.ds(sub * SSD_CHUNK, SSD_CHUNK), z_refs, xbc_refs, *rest)


def _ssd_chunk(rows, z_refs, xbc_refs, dt_ref, shift_ref, cw_ref, cb_ref, dtb_ref, alog_ref,
               dexp_ref, ng_ref, y_ref, xe_ref, rt_ref, ybuf_ref):
    lc = SSD_CHUNK
    hist = SSD_HIST

    xb = jnp.concatenate([r[rows, :] for r in xbc_refs], axis=1)
    xe_ref[hist:hist + lc, :] = xb
    xe = xe_ref[...]
    xe_ref[0:hist, :] = xb[lc - hist:lc, :]
    cw = cw_ref[...].astype(BF16)
    taps = jnp.concatenate([xe * cw[k:k + 1, :] for k in range(SSD_CONV)], axis=0)
    act = _silu(_dot(shift_ref[...], taps) + cb_ref[...])

    dt = jax.nn.softplus(dt_ref[rows, :] + dtb_ref[...])
    adt = -jnp.exp(alog_ref[...]) * dt
    ri = lax.broadcasted_iota(jnp.int32, (lc, lc), 0)
    ci = lax.broadcasted_iota(jnp.int32, (lc, lc), 1)
    lower = ri >= ci
    tril = jnp.where(lower, 1.0, 0.0)
    acs = jnp.dot(tril, adt, precision=lax.Precision.HIGHEST, preferred_element_type=F32)
    acs_t = acs.T
    dt_t = dt.T
    tot_t = acs_t[:, lc - 1:lc]
    w_t = jnp.exp(tot_t - acs_t) * dt_t
    etot_t = jnp.exp(tot_t)
    src_t = acs_t - jnp.log(dt_t)

    lane = lax.broadcasted_iota(jnp.int32, (lc, LANES), 1)
    left = lane < SSD_HEAD_DIM
    zero16 = jnp.zeros((lc, LANES), BF16)

    for g in range(SSD_GROUPS):
        b0 = SSD_INNER + g * SSD_STATE
        c0 = SSD_INNER + SSD_GROUPS * SSD_STATE + g * SSD_STATE
        cg = act[:, c0:c0 + SSD_STATE]
        bg_t = act[:, b0:b0 + SSD_STATE].T
        cbg = _dot(cg.astype(BF16), bg_t.astype(BF16))
        for pr in range(SSD_HEADS_PER_GROUP // 2):
            h0 = g * SSD_HEADS_PER_GROUP + 2 * pr
            x0 = h0 * SSD_HEAD_DIM
            xpair = act[:, x0:x0 + LANES]
            rtpair = rt_ref[:, x0:x0 + LANES]
            x16 = xpair.astype(BF16)
            r16 = rtpair.astype(BF16)
            xl, xr = jnp.where(left, x16, zero16), jnp.where(left, zero16, x16)
            rl, rr = jnp.where(left, r16, zero16), jnp.where(left, zero16, r16)
            mh, ech, wh = [], [], []
            for h in (h0, h0 + 1):
                bc = jnp.broadcast_to(acs[:, h:h + 1], (lc, lc))
                seg = jnp.where(lower, bc - src_t[h:h + 1, :], -jnp.inf)
                mh.append((cbg * jnp.exp(seg)).astype(BF16))
                ech.append((cg * jnp.exp(bc)).astype(BF16))
                wh.append((bg_t * w_t[h:h + 1, :]).astype(BF16))
            ypair = _dot(jnp.concatenate(mh + ech, axis=1),
                         jnp.concatenate([xl, xr, rl, rr], axis=0))
            spair = _dot(jnp.concatenate(wh, axis=1), jnp.concatenate([xl, xr], axis=0))
            e0 = jnp.broadcast_to(etot_t[h0:h0 + 1, :], (1, LANES))
            e1 = jnp.broadcast_to(etot_t[h0 + 1:h0 + 2, :], (1, LANES))
            rt_ref[:, x0:x0 + LANES] = rtpair * jnp.where(left[0:1], e0, e1) + spair
            yp = ypair + dexp_ref[:, x0:x0 + LANES] * xpair
            zc = x0 % PROJ_PIECE
            zpair = z_refs[x0 // PROJ_PIECE][rows, zc:zc + LANES].astype(F32)
            ybuf_ref[:, x0:x0 + LANES] = yp * _silu(zpair)

    for g in range(SSD_GROUPS):
        g0 = g * SSD_GROUP_W
        yg = ybuf_ref[:, g0:g0 + SSD_GROUP_W]
        y_ref[rows, g0:g0 + SSD_GROUP_W] = _rms(
            yg, ng_ref[:, g0:g0 + SSD_GROUP_W], SUBLN_EPS).astype(y_ref.dtype)


def _ssd(proj, dt_raw, conv_w, conv_b, dt_bias, a_log, d_exp, norm_g, batch, seq):
    t = batch * seq
    lc = SSD_CHUNK
    tr = SSD_SUB * lc
    nc = seq // tr
    row = lambda b, c: b * nc + c
    rows_e = SSD_HIST + lc
    col = jnp.arange(SSD_CONV * rows_e)
    src = SSD_HIST + jnp.arange(lc)[:, None] - (SSD_CONV - 1) + col[None, :] // rows_e
    shift = (col[None, :] % rows_e == src).astype(BF16)

    def pieces(off, width):
        return [pl.BlockSpec((tr, PROJ_PIECE), functools.partial(
            lambda b, c, blk: (row(b, c), blk), blk=(off + p0) // PROJ_PIECE))
            for p0 in range(0, width, PROJ_PIECE)]

    n_pieces = (SSD_INNER + SSD_CONV_DIM) // PROJ_PIECE
    return pl.pallas_call(
        _ssd_body,
        grid=(batch, nc),
        in_specs=pieces(PROJ_Z_OFF, SSD_INNER) + pieces(PROJ_XBC_OFF, SSD_CONV_DIM) + [
            pl.BlockSpec((tr, LANES), lambda b, c: (row(b, c), 0)),
            pl.BlockSpec((lc, SSD_CONV * rows_e), lambda b, c: (0, 0)),
            pl.BlockSpec((SSD_CONV, SSD_CONV_DIM), lambda b, c: (0, 0)),
            pl.BlockSpec((1, SSD_CONV_DIM), lambda b, c: (0, 0)),
            pl.BlockSpec((1, LANES), lambda b, c: (0, 0)),
            pl.BlockSpec((1, LANES), lambda b, c: (0, 0)),
            pl.BlockSpec((1, SSD_INNER), lambda b, c: (0, 0)),
            pl.BlockSpec((1, SSD_INNER), lambda b, c: (0, 0)),
        ],
        out_specs=pl.BlockSpec((tr, SSD_INNER), lambda b, c: (row(b, c), 0)),
        out_shape=jax.ShapeDtypeStruct((t, SSD_INNER), BF16),
        scratch_shapes=[
            pltpu.VMEM((SSD_HIST + lc, SSD_CONV_DIM), BF16),
            pltpu.VMEM((SSD_STATE, SSD_INNER), F32),
            pltpu.VMEM((lc, SSD_INNER), F32),
        ],
        compiler_params=pltpu.CompilerParams(
            dimension_semantics=("parallel", "arbitrary"), vmem_limit_bytes=VMEM_LIMIT),
        name="ssd",
    )(*([proj] * n_pieces), dt_raw, shift, conv_w, conv_b, dt_bias, a_log, d_exp, norm_g)


def _mix_xattn_body(ao_ref, ys_ref, gl_ref, bg_ref, x_ref, wa_ref, ws_ref, wm_ref, mpost_ref,
                    xpre_ref, wq_ref, kv_ref, wo_ref, xpost_ref, o_ref):
    attn_out = _dot(ao_ref[...], wa_ref[...])
    ssd_out = _dot(ys_ref[...], ws_ref[...])
    gates = jax.nn.sigmoid(gl_ref[...].astype(F32) + bg_ref[...])
    mixed = gates[:, :D_MODEL] * attn_out + gates[:, D_MODEL:] * ssd_out
    mixed = _dot(mixed, wm_ref[...])
    x = x_ref[...] + _rms(mixed, mpost_ref[...], NORM_EPS)

    hq = _rms(x, xpre_ref[...], NORM_EPS)
    qx = (_dot(hq, wq_ref[...]) * (XA_HEAD_DIM ** -0.5)).astype(BF16)
    heads = []
    for h in range(XA_HEADS):
        c0 = h * XA_HEAD_DIM
        s = _dot_nt(qx[:, c0:c0 + XA_HEAD_DIM], kv_ref[:, c0:c0 + XA_HEAD_DIM])
        e = jnp.exp(s - jnp.max(s, axis=-1, keepdims=True))
        denom = jnp.sum(e, axis=-1, keepdims=True)
        oh = _dot(e.astype(BF16), kv_ref[:, D_MODEL + c0:D_MODEL + c0 + XA_HEAD_DIM])
        heads.append(oh / denom)
    xo = _dot(jnp.concatenate(heads, axis=1), wo_ref[...])
    o_ref[...] = x + _rms(xo, xpost_ref[...], NORM_EPS)


def _mix_xattn(attn_o, y_ssd, proj, b_gate, x, w_attn, w_ssd, w_mix, mix_post_g,
               xa_pre_g, w_q, kv, w_o, xa_post_g, seq):
    t = x.shape[0]
    tm = MIX_TM
    gw = N_BRANCH * D_MODEL
    per_batch = seq // tm
    row_tile = lambda w: pl.BlockSpec((tm, w), lambda i: (i, 0))
    return pl.pallas_call(
        _mix_xattn_body,
        grid=(t // tm,),
        in_specs=[
            row_tile(DA_V),
            row_tile(SSD_INNER),
            pl.BlockSpec((tm, gw), lambda i: (i, PROJ_GATE_OFF // gw)),
            _const_spec((1, gw)),
            row_tile(D_MODEL),
            _const_spec((DA_V, D_MODEL)),
            _const_spec((SSD_INNER, D_MODEL)),
            _const_spec((D_MODEL, D_MODEL)),
            _const_spec((1, D_MODEL)),
            _const_spec((1, D_MODEL)),
            _const_spec((D_MODEL, D_MODEL)),
            pl.BlockSpec((MEM_LEN, 2 * D_MODEL), lambda i: (i // per_batch, 0)),
            _const_spec((D_MODEL, D_MODEL)),
            _const_spec((1, D_MODEL)),
        ],
        out_specs=row_tile(D_MODEL),
        out_shape=jax.ShapeDtypeStruct((t, D_MODEL), F32),
        compiler_params=pltpu.CompilerParams(
            dimension_semantics=("parallel",), vmem_limit_bytes=VMEM_LIMIT),
        name="mix_xattn",
    )(attn_o, y_ssd, proj, b_gate, x, w_attn, w_ssd, w_mix, mix_post_g,
      xa_pre_g, w_q, kv, w_o, xa_post_g)


def _row(v):
    return v.reshape(1, -1)


def _pad_lanes(v):
    return jnp.pad(v, ((0, 0), (0, LANES - v.shape[1])))


def _layer(x, mem, layer_idx, batch, seq,
           ffn1_pre_g, ffn1_post_g, ffn1_w_gu, ffn1_w_down,
           mix_pre_g, mix_post_g, w_in, b_gate,
           da_lambda_q1, da_lambda_k1, da_lambda_q2, da_lambda_k2, da_subln_g,
           ssd_conv_w, ssd_conv_b, ssd_dt_bias, ssd_A_log, ssd_D, ssd_norm_g,
           w_branch_attn, w_branch_ssd, w_mix_out,
           xa_pre_g, xa_post_g, mem_norm_g, xa_w_q, xa_w_kv, xa_w_o,
           ffn2_pre_g, ffn2_post_g, ffn2_w_gu, ffn2_w_down):
    x = _ffn(x, _row(ffn1_pre_g), ffn1_w_gu, ffn1_w_down, _row(ffn1_post_g))

    q_scale = DA_HEAD_DIM ** -0.5 * math.log2(math.e)
    proj, dt_raw = _in_proj(x, _row(mix_pre_g), w_in.T, q_scale)

    lam_init = 0.8 - 0.6 * math.exp(-0.3 * layer_idx)
    lam_params = jnp.stack([da_lambda_q1, da_lambda_k1, da_lambda_q2, da_lambda_k2])
    attn_o = _diff_attn(proj, lam_params, _row(da_subln_g), lam_init, batch, seq)

    y_ssd = _ssd(proj, dt_raw, ssd_conv_w, _row(ssd_conv_b), _pad_lanes(_row(ssd_dt_bias)),
                 _pad_lanes(_row(ssd_A_log)), _row(jnp.repeat(ssd_D, SSD_HEAD_DIM)),
                 _row(ssd_norm_g), batch, seq)

    mem2 = mem.reshape(batch * MEM_LEN, D_MODEL)
    kv = _norm_proj(mem2, _row(mem_norm_g), xa_w_kv, MEM_LEN)
    x = _mix_xattn(attn_o, y_ssd, proj, _row(b_gate), x, w_branch_attn, w_branch_ssd, w_mix_out,
                   _row(mix_post_g), _row(xa_pre_g), xa_w_q, kv, xa_w_o, _row(xa_post_g), seq)

    x = _ffn(x, _row(ffn2_pre_g), ffn2_w_gu, ffn2_w_down, _row(ffn2_post_g))
    return x


def kernel(x, mem, ffn1_pre_g, ffn1_post_g, ffn1_w_gu, ffn1_w_down, mix_pre_g, mix_post_g, w_in, b_gate, da_lambda_q1, da_lambda_k1, da_lambda_q2, da_lambda_k2, da_subln_g, ssd_conv_w, ssd_conv_b, ssd_dt_bias, ssd_A_log, ssd_D, ssd_norm_g, w_branch_attn, w_branch_ssd, w_mix_out, xa_pre_g, xa_post_g, mem_norm_g, xa_w_q, xa_w_kv, xa_w_o, ffn2_pre_g, ffn2_post_g, ffn2_w_gu, ffn2_w_down):
    batch, seq, d = x.shape
    params = (ffn1_pre_g, ffn1_post_g, ffn1_w_gu, ffn1_w_down, mix_pre_g, mix_post_g, w_in, b_gate,
              da_lambda_q1, da_lambda_k1, da_lambda_q2, da_lambda_k2, da_subln_g,
              ssd_conv_w, ssd_conv_b, ssd_dt_bias, ssd_A_log, ssd_D, ssd_norm_g,
              w_branch_attn, w_branch_ssd, w_mix_out,
              xa_pre_g, xa_post_g, mem_norm_g, xa_w_q, xa_w_kv, xa_w_o,
              ffn2_pre_g, ffn2_post_g, ffn2_w_gu, ffn2_w_down)
    h = x.reshape(batch * seq, d)
    for layer in range(ffn1_pre_g.shape[0]):
        h = _layer(h, mem, layer, batch, seq, *[p[layer] for p in params])
    return h.reshape(batch, seq, d)
```

```python
import functools
import math

import jax
import jax.numpy as jnp
from jax import lax
from jax.experimental import pallas as pl
from jax.experimental.pallas import tpu as pltpu

F32 = jnp.float32
BF16 = jnp.bfloat16

D_MODEL = 1024
MEM_LEN = 256
DA_HEADS = 8
DA_HEAD_DIM = 64
DA_PAIR = 2 * DA_HEAD_DIM
DA_QK = DA_HEADS * DA_PAIR
DA_V = DA_HEADS * DA_PAIR
SSD_INNER = 2 * D_MODEL
SSD_HEAD_DIM = 64
SSD_HEADS = SSD_INNER // SSD_HEAD_DIM
SSD_GROUPS = 4
SSD_HEADS_PER_GROUP = SSD_HEADS // SSD_GROUPS
SSD_STATE = 128
SSD_CONV = 4
SSD_CHUNK = 128
SSD_GROUP_W = SSD_INNER // SSD_GROUPS
SSD_SUB = 4
SSD_HIST = 16
SSD_CONV_DIM = SSD_INNER + 2 * SSD_GROUPS * SSD_STATE
XA_HEADS = 4
XA_HEAD_DIM = D_MODEL // XA_HEADS
D_FF = 2816
N_BRANCH = 2
NORM_EPS = 1e-6
SUBLN_EPS = 1e-5

LANES = 128
SUBLANES = 8
VMEM_LIMIT = 56 * 1024 * 1024

PROJ_Z_OFF = 3 * DA_QK
PROJ_XBC_OFF = PROJ_Z_OFF + SSD_INNER
PROJ_GATE_OFF = PROJ_XBC_OFF + SSD_CONV_DIM
PROJ_WIDTH = PROJ_GATE_OFF + N_BRANCH * D_MODEL
PROJ_PIECE = 1024

FFN_TM = 512
FFN_CHUNK = 256
PROJ_TM = 1024
PROJ_TN = 2048
PROJ_MAIN_TILES = PROJ_GATE_OFF // PROJ_TN
ATTN_TQ = 512
ATTN_KSLAB = 256
ATTN_HPS = 2
MIX_TM = 512


def _rms(x, g, eps):
    return x * lax.rsqrt(jnp.mean(x * x, axis=-1, keepdims=True) + eps) * g


def _silu(x):
    h = 0.5 * x
    return h + h * jnp.tanh(h)


def _dot(a, b):
    return jnp.dot(a, b, preferred_element_type=F32)


def _dot_nt(a, b):
    return lax.dot_general(a, b, (((1,), (1,)), ((), ())), preferred_element_type=F32)


def _const_spec(shape):
    nd = len(shape)
    return pl.BlockSpec(shape, lambda *_: (0,) * nd, pipeline_mode=pl.Buffered(1))


def _ffn_body(x_ref, pre_ref, wgu_ref, wd_ref, post_ref, o_ref):
    x = x_ref[...]
    hn = _rms(x, pre_ref[...], NORM_EPS)
    acc = None
    for c0 in range(0, D_FF, FFN_CHUNK):
        g = _dot(hn, wgu_ref[:, c0:c0 + FFN_CHUNK])
        u = _dot(hn, wgu_ref[:, D_FF + c0:D_FF + c0 + FFN_CHUNK])
        a = _silu(g) * u
        d = _dot(a, wd_ref[c0:c0 + FFN_CHUNK, :])
        acc = d if acc is None else acc + d
    o_ref[...] = x + 0.5 * _rms(acc, post_ref[...], NORM_EPS)


def _ffn(x, pre_g, w_gu, w_down, post_g):
    t = x.shape[0]
    return pl.pallas_call(
        _ffn_body,
        grid=(t // FFN_TM,),
        in_specs=[
            pl.BlockSpec((FFN_TM, D_MODEL), lambda i: (i, 0)),
            _const_spec((1, D_MODEL)),
            _const_spec((D_MODEL, 2 * D_FF)),
            _const_spec((D_FF, D_MODEL)),
            _const_spec((1, D_MODEL)),
        ],
        out_specs=pl.BlockSpec((FFN_TM, D_MODEL), lambda i: (i, 0)),
        out_shape=jax.ShapeDtypeStruct((t, D_MODEL), F32),
        compiler_params=pltpu.CompilerParams(
            dimension_semantics=("parallel",), vmem_limit_bytes=VMEM_LIMIT),
        name="ffn",
    )(x, pre_g, w_gu, w_down, post_g)


def _norm_proj_body(x_ref, g_ref, w_ref, o_ref):
    hn = _rms(x_ref[...], g_ref[...], NORM_EPS)
    o_ref[...] = _dot(hn, w_ref[...]).astype(o_ref.dtype)


def _norm_proj(x, g, w, tm):
    t, n = x.shape[0], w.shape[1]
    return pl.pallas_call(
        _norm_proj_body,
        grid=(t // tm,),
        in_specs=[
            pl.BlockSpec((tm, D_MODEL), lambda i: (i, 0)),
            _const_spec((1, D_MODEL)),
            _const_spec((D_MODEL, n)),
        ],
        out_specs=pl.BlockSpec((tm, n), lambda i: (i, 0)),
        out_shape=jax.ShapeDtypeStruct((t, n), BF16),
        compiler_params=pltpu.CompilerParams(
            dimension_semantics=("parallel",), vmem_limit_bytes=VMEM_LIMIT),
        name="norm_proj",
    )(x, g, w)


def _in_proj_body(q_scale, x_ref, g_ref, w_ref, wt_ref, o_ref, dt_ref):
    j = pl.program_id(0)
    tn = o_ref.shape[1]

    @pl.when(j < PROJ_MAIN_TILES)
    def _():
        hn = _rms(x_ref[...], g_ref[...], NORM_EPS)
        col = j * tn + lax.broadcasted_iota(jnp.int32, (1, tn), 1)
        scale = jnp.where(col < DA_QK, q_scale, 1.0)
        o_ref[...] = (_dot_nt(hn, w_ref[...]) * scale).astype(o_ref.dtype)

    @pl.when(j == PROJ_MAIN_TILES)
    def _():
        hn = _rms(x_ref[...], g_ref[...], NORM_EPS)
        tail = _dot_nt(hn, wt_ref[...])
        dt_ref[...] = tail[:, :LANES]
        o_ref[...] = tail[:, SSD_HEADS:SSD_HEADS + tn].astype(o_ref.dtype)


def _in_proj(x, g, w_t, q_scale):
    t = x.shape[0]
    tm, tn = PROJ_TM, PROJ_TN
    assert w_t.shape[0] == PROJ_MAIN_TILES * tn + SSD_HEADS + tn
    return pl.pallas_call(
        functools.partial(_in_proj_body, q_scale),
        grid=(PROJ_MAIN_TILES + 1, t // tm),
        in_specs=[
            pl.BlockSpec((tm, D_MODEL), lambda j, i: (i, 0)),
            pl.BlockSpec((1, D_MODEL), lambda j, i: (0, 0)),
            pl.BlockSpec((tn, D_MODEL), lambda j, i: (jnp.minimum(j, PROJ_MAIN_TILES - 1), 0)),
            pl.BlockSpec((pl.Element(SSD_HEADS + tn), pl.Element(D_MODEL)),
                         lambda j, i: (PROJ_MAIN_TILES * tn, 0), pipeline_mode=pl.Buffered(1)),
        ],
        out_specs=[
            pl.BlockSpec((tm, tn), lambda j, i: (i, j)),
            pl.BlockSpec((tm, LANES), lambda j, i: (jnp.where(j == PROJ_MAIN_TILES, i, 0), 0)),
        ],
        out_shape=[
            jax.ShapeDtypeStruct((t, PROJ_WIDTH), BF16),
            jax.ShapeDtypeStruct((t, LANES), F32),
        ],
        compiler_params=pltpu.CompilerParams(
            dimension_semantics=("arbitrary", "arbitrary"), vmem_limit_bytes=VMEM_LIMIT),
        name="in_proj",
    )(x, g, w_t, w_t)


def _attn_body(lam_init, seq, q_ref, k_ref, v_ref, lam_ref, g_ref, o_ref,
               qq_ref, m_ref, acc_ref, vx_ref, mask_ref):
    tq = ATTN_TQ
    heads = range(ATTN_HPS)
    lamv = lam_ref[...]
    lam = (jnp.exp(jnp.sum(lamv[0:1] * lamv[1:2], axis=-1, keepdims=True))
           - jnp.exp(jnp.sum(lamv[2:3] * lamv[3:4], axis=-1, keepdims=True)) + lam_init)
    lane = lax.broadcasted_iota(jnp.int32, (1, DA_PAIR), 1)
    first_half = lane < DA_HEAD_DIM
    ones_col = jnp.broadcast_to(jnp.where(lane == 0, 1.0, 0.0), (seq, DA_PAIR)).astype(BF16)
    for hh in heads:
        vx_ref[hh, :, :DA_PAIR] = v_ref[:, hh * DA_PAIR:(hh + 1) * DA_PAIR]
        vx_ref[hh, :, DA_PAIR:] = ones_col

    row = lax.broadcasted_iota(jnp.int32, (2 * tq, tq), 0)
    col = lax.broadcasted_iota(jnp.int32, (2 * tq, tq), 1)
    mask_ref[...] = jnp.where(jnp.where(row >= tq, row - tq, row) >= col, 0.0, -jnp.inf)

    def kv_block(slot, k0, width, first):
        def slab(hh, c):
            s = _dot_nt(qq_ref[hh], k_ref[pl.ds(pl.multiple_of(k0 + c, ATTN_KSLAB), ATTN_KSLAB),
                                          hh * DA_PAIR:(hh + 1) * DA_PAIR])
            d = c - (width - tq)
            return s + mask_ref[:, d:d + ATTN_KSLAB] if first and d >= 0 else s

        scores = [jnp.concatenate([slab(hh, c) for c in range(0, width, ATTN_KSLAB)], axis=1)
                  for hh in heads]
        for hh, s in zip(heads, scores):
            m_new = jnp.broadcast_to(jnp.max(s, axis=-1, keepdims=True), (2 * tq, LANES))
            if not first:
                m_old = m_ref[slot, hh]
                m_new = jnp.maximum(m_old, m_new)
                alpha = jnp.exp2(m_old - m_new)
                alpha2 = jnp.concatenate([alpha, alpha], axis=1)
            p = jnp.concatenate(
                [jnp.exp2(s[:, c:c + LANES] - m_new).astype(BF16) for c in range(0, width, LANES)],
                axis=1)
            vb = vx_ref[hh, pl.ds(k0, width), :]
            for r in (0, tq):
                pv = _dot(p[r:r + tq], vb)
                if first:
                    acc_ref[slot, hh, r:r + tq, :] = pv
                else:
                    acc_ref[slot, hh, r:r + tq, :] = (
                        acc_ref[slot, hh, r:r + tq, :] * alpha2[r:r + tq] + pv)
            m_ref[slot, hh] = m_new

    def load_queries(r0):
        for hh in heads:
            q = q_ref[pl.ds(r0, tq), hh * DA_PAIR:(hh + 1) * DA_PAIR]
            qq_ref[hh, 0:tq, :] = jnp.where(first_half, q, jnp.zeros_like(q))
            qq_ref[hh, tq:2 * tq, :] = jnp.where(first_half, jnp.zeros_like(q), q)

    def finish(slot, r0):
        for hh in heads:
            acc = acc_ref[slot, hh]
            o = acc[:, :DA_PAIR] / acc[:, DA_PAIR:DA_PAIR + 1]
            res = o[:tq] - lam * o[tq:]
            out = _rms(res, g_ref[...], SUBLN_EPS) * (1.0 - lam_init)
            o_ref[pl.ds(r0, tq), hh * DA_PAIR:(hh + 1) * DA_PAIR] = out.astype(o_ref.dtype)

    def earlier_keys(slot, a):
        def double_step(jj, c):
            kv_block(slot, pl.multiple_of(jj * (2 * tq), 2 * tq), 2 * tq, False)
            return c

        lax.fori_loop(0, a, double_step, 0)

    def tile_pair(a, carry):
        r_even = pl.multiple_of(a * (2 * tq), 2 * tq)
        r_odd = pl.multiple_of(r_even + tq, tq)
        load_queries(r_even)
        kv_block(0, r_even, tq, True)
        earlier_keys(0, a)
        load_queries(r_odd)
        kv_block(1, r_even, 2 * tq, True)
        finish(0, r_even)
        earlier_keys(1, a)
        finish(1, r_odd)
        return carry

    lax.fori_loop(0, seq // (2 * tq), tile_pair, 0)


def _diff_attn(proj, lam_params, subln_g, lam_init, batch, seq):
    t = batch * seq
    tq = ATTN_TQ
    w = ATTN_HPS * DA_PAIR
    steps = DA_HEADS // ATTN_HPS
    return pl.pallas_call(
        functools.partial(_attn_body, lam_init, seq),
        grid=(batch, steps),
        in_specs=[
            pl.BlockSpec((seq, w), lambda b, h: (b, h)),
            pl.BlockSpec((seq, w), lambda b, h: (b, steps + h)),
            pl.BlockSpec((seq, w), lambda b, h: (b, 2 * steps + h)),
            pl.BlockSpec((4, DA_HEAD_DIM), lambda b, h: (0, 0)),
            pl.BlockSpec((1, DA_PAIR), lambda b, h: (0, 0)),
        ],
        out_specs=pl.BlockSpec((seq, w), lambda b, h: (b, h)),
        out_shape=jax.ShapeDtypeStruct((t, DA_V), BF16),
        scratch_shapes=[
            pltpu.VMEM((ATTN_HPS, 2 * tq, DA_PAIR), BF16),
            pltpu.VMEM((2, ATTN_HPS, 2 * tq, LANES), F32),
            pltpu.VMEM((2, ATTN_HPS, 2 * tq, 2 * DA_PAIR), F32),
            pltpu.VMEM((ATTN_HPS, seq, 2 * DA_PAIR), BF16),
            pltpu.VMEM((2 * tq, tq), F32),
        ],
        compiler_params=pltpu.CompilerParams(
            dimension_semantics=("parallel", "parallel"), vmem_limit_bytes=VMEM_LIMIT),
        name="diff_attn",
    )(proj, proj, proj, lam_params, subln_g)


def _ssd_body(*refs):
    nz, nx = SSD_INNER // PROJ_PIECE, SSD_CONV_DIM // PROJ_PIECE
    z_refs, xbc_refs = refs[:nz], refs[nz:nz + nx]
    rest = refs[nz + nx:]
    xe_ref, rt_ref = rest[-3], rest[-2]

    @pl.when(pl.program_id(1) == 0)
    def _():
        xe_ref[0:SSD_HIST, :] = jnp.zeros((SSD_HIST, SSD_CONV_DIM), BF16)
        rt_ref[...] = jnp.zeros(rt_ref.shape, F32)

    for sub in range(SSD_SUB):
        _ssd_chunk(pl.ds(sub * SSD_CHUNK, SSD_CHUNK), z_refs, xbc_refs, *rest)


def _ssd_chunk(rows, z_refs, xbc_refs, dt_ref, shift_ref, cw_ref, cb_ref, dtb_ref, alog_ref,
               dexp_ref, ng_ref, y_ref, xe_ref, rt_ref, ybuf_ref):
    lc = SSD_CHUNK
    hist = SSD_HIST

    xb = jnp.concatenate([r[rows, :] for r in xbc_refs], axis=1)
    xe_ref[hist:hist + lc, :] = xb
    xe = xe_ref[...]
    xe_ref[0:hist, :] = xb[lc - hist:lc, :]
    cw = cw_ref[...].astype(BF16)
    taps = jnp.concatenate([xe * cw[k:k + 1, :] for k in range(SSD_CONV)], axis=0)
    act = _silu(_dot(shift_ref[...], taps) + cb_ref[...])

    dt = jax.nn.softplus(dt_ref[rows, :] + dtb_ref[...])
    adt = -jnp.exp(alog_ref[...]) * dt
    ri = lax.broadcasted_iota(jnp.int32, (lc, lc), 0)
    ci = lax.broadcasted_iota(jnp.int32, (lc, lc), 1)
    lower = ri >= ci
    tril = jnp.where(lower, 1.0, 0.0)
    acs = jnp.dot(tril, adt, precision=lax.Precision.HIGHEST, preferred_element_type=F32)
    acs_t = acs.T
    dt_t = dt.T
    tot_t = acs_t[:, lc - 1:lc]
    w_t = jnp.exp(tot_t - acs_t) * dt_t
    etot_t = jnp.exp(tot_t)
    src_t = acs_t - jnp.log(dt_t)

    lane = lax.broadcasted_iota(jnp.int32, (lc, LANES), 1)
    left = lane < SSD_HEAD_DIM
    zero16 = jnp.zeros((lc, LANES), BF16)

    for g in range(SSD_GROUPS):
        b0 = SSD_INNER + g * SSD_STATE
        c0 = SSD_INNER + SSD_GROUPS * SSD_STATE + g * SSD_STATE
        cg = act[:, c0:c0 + SSD_STATE]
        bg_t = act[:, b0:b0 + SSD_STATE].T
        cg16 = cg.astype(BF16)
        bgt16 = bg_t.astype(BF16)
        cbg16 = _dot(cg16, bgt16).astype(BF16)
        for pr in range(SSD_HEADS_PER_GROUP // 2):
            h0 = g * SSD_HEADS_PER_GROUP + 2 * pr
            x0 = h0 * SSD_HEAD_DIM
            xpair = act[:, x0:x0 + LANES]
            rtpair = rt_ref[:, x0:x0 + LANES]
            x16 = xpair.astype(BF16)
            r16 = rtpair.astype(BF16)
            xl, xr = jnp.where(left, x16, zero16), jnp.where(left, zero16, x16)
            rl, rr = jnp.where(left, r16, zero16), jnp.where(left, zero16, r16)
            mh, ech, wh = [], [], []
            for h in (h0, h0 + 1):
                bc = jnp.broadcast_to(acs[:, h:h + 1], (lc, lc))
                seg = jnp.where(lower, bc - src_t[h:h + 1, :], -jnp.inf)
                mh.append(cbg16 * jnp.exp(seg).astype(BF16))
                ech.append(cg16 * jnp.exp(bc).astype(BF16))
                wh.append(bgt16 * jnp.broadcast_to(w_t[h:h + 1, :], (SSD_STATE, lc)).astype(BF16))
            ypair = _dot(jnp.concatenate(mh + ech, axis=1),
                         jnp.concatenate([xl, xr, rl, rr], axis=0))
            spair = _dot(jnp.concatenate(wh, axis=1), jnp.concatenate([xl, xr], axis=0))
            e0 = jnp.broadcast_to(etot_t[h0:h0 + 1, :], (1, LANES))
            e1 = jnp.broadcast_to(etot_t[h0 + 1:h0 + 2, :], (1, LANES))
            rt_ref[:, x0:x0 + LANES] = rtpair * jnp.where(left[0:1], e0, e1) + spair
            yp = ypair + dexp_ref[:, x0:x0 + LANES] * xpair
            zc = x0 % PROJ_PIECE
            zpair = z_refs[x0 // PROJ_PIECE][rows, zc:zc + LANES].astype(F32)
            ybuf_ref[:, x0:x0 + LANES] = yp * _silu(zpair)

    for g in range(SSD_GROUPS):
        g0 = g * SSD_GROUP_W
        yg = ybuf_ref[:, g0:g0 + SSD_GROUP_W]
        y_ref[rows, g0:g0 + SSD_GROUP_W] = _rms(
            yg, ng_ref[:, g0:g0 + SSD_GROUP_W], SUBLN_EPS).astype(y_ref.dtype)


def _ssd(proj, dt_raw, conv_w, conv_b, dt_bias, a_log, d_exp, norm_g, batch, seq):
    t = batch * seq
    lc = SSD_CHUNK
    tr = SSD_SUB * lc
    nc = seq // tr
    row = lambda b, c: b * nc + c
    rows_e = SSD_HIST + lc
    col = jnp.arange(SSD_CONV * rows_e)
    src = SSD_HIST + jnp.arange(lc)[:, None] - (SSD_CONV - 1) + col[None, :] // rows_e
    shift = (col[None, :] % rows_e == src).astype(BF16)

    def pieces(off, width):
        return [pl.BlockSpec((tr, PROJ_PIECE), functools.partial(
            lambda b, c, blk: (row(b, c), blk), blk=(off + p0) // PROJ_PIECE))
            for p0 in range(0, width, PROJ_PIECE)]

    n_pieces = (SSD_INNER + SSD_CONV_DIM) // PROJ_PIECE
    return pl.pallas_call(
        _ssd_body,
        grid=(batch, nc),
        in_specs=pieces(PROJ_Z_OFF, SSD_INNER) + pieces(PROJ_XBC_OFF, SSD_CONV_DIM) + [
            pl.BlockSpec((tr, LANES), lambda b, c: (row(b, c), 0)),
            pl.BlockSpec((lc, SSD_CONV * rows_e), lambda b, c: (0, 0)),
            pl.BlockSpec((SSD_CONV, SSD_CONV_DIM), lambda b, c: (0, 0)),
            pl.BlockSpec((1, SSD_CONV_DIM), lambda b, c: (0, 0)),
            pl.BlockSpec((1, LANES), lambda b, c: (0, 0)),
            pl.BlockSpec((1, LANES), lambda b, c: (0, 0)),
            pl.BlockSpec((1, SSD_INNER), lambda b, c: (0, 0)),
            pl.BlockSpec((1, SSD_INNER), lambda b, c: (0, 0)),
        ],
        out_specs=pl.BlockSpec((tr, SSD_INNER), lambda b, c: (row(b, c), 0)),
        out_shape=jax.ShapeDtypeStruct((t, SSD_INNER), BF16),
        scratch_shapes=[
            pltpu.VMEM((SSD_HIST + lc, SSD_CONV_DIM), BF16),
            pltpu.VMEM((SSD_STATE, SSD_INNER), F32),
            pltpu.VMEM((lc, SSD_INNER), F32),
        ],
        compiler_params=pltpu.CompilerParams(
            dimension_semantics=("parallel", "arbitrary"), vmem_limit_bytes=VMEM_LIMIT),
        name="ssd",
    )(*([proj] * n_pieces), dt_raw, shift, conv_w, conv_b, dt_bias, a_log, d_exp, norm_g)


def _mix_xattn_body(ao_ref, ys_ref, gl_ref, bg_ref, x_ref, wa_ref, ws_ref, wm_ref, mpost_ref,
                    xpre_ref, wq_ref, kv_ref, wo_ref, xpost_ref, o_ref):
    attn_out = _dot(ao_ref[...], wa_ref[...])
    ssd_out = _dot(ys_ref[...], ws_ref[...])
    gates = jax.nn.sigmoid(gl_ref[...].astype(F32) + bg_ref[...])
    mixed = gates[:, :D_MODEL] * attn_out + gates[:, D_MODEL:] * ssd_out
    mixed = _dot(mixed, wm_ref[...])
    x = x_ref[...] + _rms(mixed, mpost_ref[...], NORM_EPS)

    hq = _rms(x, xpre_ref[...], NORM_EPS)
    qx = (_dot(hq, wq_ref[...]) * (XA_HEAD_DIM ** -0.5)).astype(BF16)
    heads = []
    for h in range(XA_HEADS):
        c0 = h * XA_HEAD_DIM
        s = _dot_nt(qx[:, c0:c0 + XA_HEAD_DIM], kv_ref[:, c0:c0 + XA_HEAD_DIM])
        e = jnp.exp(s - jnp.max(s, axis=-1, keepdims=True))
        denom = jnp.sum(e, axis=-1, keepdims=True)
        oh = _dot(e.astype(BF16), kv_ref[:, D_MODEL + c0:D_MODEL + c0 + XA_HEAD_DIM])
        heads.append(oh / denom)
    xo = _dot(jnp.concatenate(heads, axis=1), wo_ref[...])
    o_ref[...] = x + _rms(xo, xpost_ref[...], NORM_EPS)


def _mix_xattn(attn_o, y_ssd, proj, b_gate, x, w_attn, w_ssd, w_mix, mix_post_g,
               xa_pre_g, w_q, kv, w_o, xa_post_g, seq):
    t = x.shape[0]
    tm = MIX_TM
    gw = N_BRANCH * D_MODEL
    per_batch = seq // tm
    row_tile = lambda w: pl.BlockSpec((tm, w), lambda i: (i, 0))
    return pl.pallas_call(
        _mix_xattn_body,
        grid=(t // tm,),
        in_specs=[
            row_tile(DA_V),
            row_tile(SSD_INNER),
            pl.BlockSpec((tm, gw), lambda i: (i, PROJ_GATE_OFF // gw)),
            _const_spec((1, gw)),
            row_tile(D_MODEL),
            _const_spec((DA_V, D_MODEL)),
            _const_spec((SSD_INNER, D_MODEL)),
            _const_spec((D_MODEL, D_MODEL)),
            _const_spec((1, D_MODEL)),
            _const_spec((1, D_MODEL)),
            _const_spec((D_MODEL, D_MODEL)),
            pl.BlockSpec((MEM_LEN, 2 * D_MODEL), lambda i: (i // per_batch, 0)),
            _const_spec((D_MODEL, D_MODEL)),
            _const_spec((1, D_MODEL)),
        ],
        out_specs=row_tile(D_MODEL),
        out_shape=jax.ShapeDtypeStruct((t, D_MODEL), F32),
        compiler_params=pltpu.CompilerParams(
            dimension_semantics=("parallel",), vmem_limit_bytes=VMEM_LIMIT),
        name="mix_xattn",
    )(attn_o, y_ssd, proj, b_gate, x, w_attn, w_ssd, w_mix, mix_post_g,
      xa_pre_g, w_q, kv, w_o, xa_post_g)


def _row(v):
    return v.reshape(1, -1)


def _pad_lanes(v):
    return jnp.pad(v, ((0, 0), (0, LANES - v.shape[1])))


def _layer(x, mem, layer_idx, batch, seq,
           ffn1_pre_g, ffn1_post_g, ffn1_w_gu, ffn1_w_down,
           mix_pre_g, mix_post_g, w_in, b_gate,
           da_lambda_q1, da_lambda_k1, da_lambda_q2, da_lambda_k2, da_subln_g,
           ssd_conv_w, ssd_conv_b, ssd_dt_bias, ssd_A_log, ssd_D, ssd_norm_g,
           w_branch_attn, w_branch_ssd, w_mix_out,
           xa_pre_g, xa_post_g, mem_norm_g, xa_w_q, xa_w_kv, xa_w_o,
           ffn2_pre_g, ffn2_post_g, ffn2_w_gu, ffn2_w_down):
    x = _ffn(x, _row(ffn1_pre_g), ffn1_w_gu, ffn1_w_down, _row(ffn1_post_g))

    q_scale = DA_HEAD_DIM ** -0.5 * math.log2(math.e)
    proj, dt_raw = _in_proj(x, _row(mix_pre_g), w_in.T, q_scale)

    lam_init = 0.8 - 0.6 * math.exp(-0.3 * layer_idx)
    lam_params = jnp.stack([da_lambda_q1, da_lambda_k1, da_lambda_q2, da_lambda_k2])
    attn_o = _diff_attn(proj, lam_params, _row(da_subln_g), lam_init, batch, seq)

    y_ssd = _ssd(proj, dt_raw, ssd_conv_w, _row(ssd_conv_b), _pad_lanes(_row(ssd_dt_bias)),
                 _pad_lanes(_row(ssd_A_log)), _row(jnp.repeat(ssd_D, SSD_HEAD_DIM)),
                 _row(ssd_norm_g), batch, seq)

    mem2 = mem.reshape(batch * MEM_LEN, D_MODEL)
    kv = _norm_proj(mem2, _row(mem_norm_g), xa_w_kv, MEM_LEN)
    x = _mix_xattn(attn_o, y_ssd, proj, _row(b_gate), x, w_branch_attn, w_branch_ssd, w_mix_out,
                   _row(mix_post_g), _row(xa_pre_g), xa_w_q, kv, xa_w_o, _row(xa_post_g), seq)

    x = _ffn(x, _row(ffn2_pre_g), ffn2_w_gu, ffn2_w_down, _row(ffn2_post_g))
    return x


def kernel(x, mem, ffn1_pre_g, ffn1_post_g, ffn1_w_gu, ffn1_w_down, mix_pre_g, mix_post_g, w_in, b_gate, da_lambda_q1, da_lambda_k1, da_lambda_q2, da_lambda_k2, da_subln_g, ssd_conv_w, ssd_conv_b, ssd_dt_bias, ssd_A_log, ssd_D, ssd_norm_g, w_branch_attn, w_branch_ssd, w_mix_out, xa_pre_g, xa_post_g, mem_norm_g, xa_w_q, xa_w_kv, xa_w_o, ffn2_pre_g, ffn2_post_g, ffn2_w_gu, ffn2_w_down):
    batch, seq, d = x.shape
    params = (ffn1_pre_g, ffn1_post_g, ffn1_w_gu, ffn1_w_down, mix_pre_g, mix_post_g, w_in, b_gate,
              da_lambda_q1, da_lambda_k1, da_lambda_q2, da_lambda_k2, da_subln_g,
              ssd_conv_w, ssd_conv_b, ssd_dt_bias, ssd_A_log, ssd_D, ssd_norm_g,
              w_branch_attn, w_branch_ssd, w_mix_out,
              xa_pre_g, xa_post_g, mem_norm_g, xa_w_q, xa_w_kv, xa_w_o,
              ffn2_pre_g, ffn2_post_g, ffn2_w_gu, ffn2_w_down)
    h = x.reshape(batch * seq, d)
    for layer in range(ffn1_pre_g.shape[0]):
        h = _layer(h, mem, layer, batch, seq, *[p[layer] for p in params])
    return h.reshape(batch, seq, d)
```

```python
import functools
import math

import jax
import jax.numpy as jnp
from jax import lax
from jax.experimental import pallas as pl
from jax.experimental.pallas import tpu as pltpu

F32 = jnp.float32
BF16 = jnp.bfloat16

D_MODEL = 1024
MEM_LEN = 256
DA_HEADS = 8
DA_HEAD_DIM = 64
DA_PAIR = 2 * DA_HEAD_DIM
DA_QK = DA_HEADS * DA_PAIR
DA_V = DA_HEADS * DA_PAIR
SSD_INNER = 2 * D_MODEL
SSD_HEAD_DIM = 64
SSD_HEADS = SSD_INNER // SSD_HEAD_DIM
SSD_GROUPS = 4
SSD_HEADS_PER_GROUP = SSD_HEADS // SSD_GROUPS
SSD_STATE = 128
SSD_CONV = 4
SSD_CHUNK = 128
SSD_GROUP_W = SSD_INNER // SSD_GROUPS
SSD_SUB = 4
SSD_HIST = 16
SSD_CONV_DIM = SSD_INNER + 2 * SSD_GROUPS * SSD_STATE
XA_HEADS = 4
XA_HEAD_DIM = D_MODEL // XA_HEADS
D_FF = 2816
N_BRANCH = 2
NORM_EPS = 1e-6
SUBLN_EPS = 1e-5

LANES = 128
SUBLANES = 8
VMEM_LIMIT = 56 * 1024 * 1024

PROJ_Z_OFF = 3 * DA_QK
PROJ_XBC_OFF = PROJ_Z_OFF + SSD_INNER
PROJ_GATE_OFF = PROJ_XBC_OFF + SSD_CONV_DIM
PROJ_WIDTH = PROJ_GATE_OFF + N_BRANCH * D_MODEL
PROJ_PIECE = 1024

FFN_TM = 512
FFN_CHUNK = 256
PROJ_TM = 1024
PROJ_TN = 2048
PROJ_MAIN_TILES = PROJ_GATE_OFF // PROJ_TN
ATTN_TQ = 512
ATTN_KSLAB = 256
ATTN_HPS = 2
MIX_TM = 512


def _rms(x, g, eps):
    return x * lax.rsqrt(jnp.mean(x * x, axis=-1, keepdims=True) + eps) * g


def _silu(x):
    h = 0.5 * x
    return h + h * jnp.tanh(h)


def _dot(a, b):
    return jnp.dot(a, b, preferred_element_type=F32)


def _dot_nt(a, b):
    return lax.dot_general(a, b, (((1,), (1,)), ((), ())), preferred_element_type=F32)


def _const_spec(shape):
    nd = len(shape)
    return pl.BlockSpec(shape, lambda *_: (0,) * nd, pipeline_mode=pl.Buffered(1))


def _ffn_body(x_ref, pre_ref, wgu_ref, wd_ref, post_ref, o_ref):
    x = x_ref[...]
    hn = _rms(x, pre_ref[...], NORM_EPS)
    acc = None
    for c0 in range(0, D_FF, FFN_CHUNK):
        g = _dot(hn, wgu_ref[:, c0:c0 + FFN_CHUNK])
        u = _dot(hn, wgu_ref[:, D_FF + c0:D_FF + c0 + FFN_CHUNK])
        a = _silu(g) * u
        d = _dot(a, wd_ref[c0:c0 + FFN_CHUNK, :])
        acc = d if acc is None else acc + d
    o_ref[...] = x + 0.5 * _rms(acc, post_ref[...], NORM_EPS)


def _ffn(x, pre_g, w_gu, w_down, post_g):
    t = x.shape[0]
    return pl.pallas_call(
        _ffn_body,
        grid=(t // FFN_TM,),
        in_specs=[
            pl.BlockSpec((FFN_TM, D_MODEL), lambda i: (i, 0)),
            _const_spec((1, D_MODEL)),
            _const_spec((D_MODEL, 2 * D_FF)),
            _const_spec((D_FF, D_MODEL)),
            _const_spec((1, D_MODEL)),
        ],
        out_specs=pl.BlockSpec((FFN_TM, D_MODEL), lambda i: (i, 0)),
        out_shape=jax.ShapeDtypeStruct((t, D_MODEL), F32),
        compiler_params=pltpu.CompilerParams(
            dimension_semantics=("parallel",), vmem_limit_bytes=VMEM_LIMIT),
        name="ffn",
    )(x, pre_g, w_gu, w_down, post_g)


def _norm_proj_body(x_ref, g_ref, w_ref, o_ref):
    hn = _rms(x_ref[...], g_ref[...], NORM_EPS)
    o_ref[...] = _dot(hn, w_ref[...]).astype(o_ref.dtype)


def _norm_proj(x, g, w, tm):
    t, n = x.shape[0], w.shape[1]
    return pl.pallas_call(
        _norm_proj_body,
        grid=(t // tm,),
        in_specs=[
            pl.BlockSpec((tm, D_MODEL), lambda i: (i, 0)),
            _const_spec((1, D_MODEL)),
            _const_spec((D_MODEL, n)),
        ],
        out_specs=pl.BlockSpec((tm, n), lambda i: (i, 0)),
        out_shape=jax.ShapeDtypeStruct((t, n), BF16),
        compiler_params=pltpu.CompilerParams(
            dimension_semantics=("parallel",), vmem_limit_bytes=VMEM_LIMIT),
        name="norm_proj",
    )(x, g, w)


def _in_proj_body(q_scale, x_ref, g_ref, w_ref, wt_ref, o_ref, dt_ref):
    j = pl.program_id(0)
    tn = o_ref.shape[1]

    @pl.when(j < PROJ_MAIN_TILES)
    def _():
        hn = _rms(x_ref[...], g_ref[...], NORM_EPS)
        col = j * tn + lax.broadcasted_iota(jnp.int32, (1, tn), 1)
        scale = jnp.where(col < DA_QK, q_scale, 1.0)
        o_ref[...] = (_dot_nt(hn, w_ref[...]) * scale).astype(o_ref.dtype)

    @pl.when(j == PROJ_MAIN_TILES)
    def _():
        hn = _rms(x_ref[...], g_ref[...], NORM_EPS)
        tail = _dot_nt(hn, wt_ref[...])
        dt_ref[...] = tail[:, :LANES]
        o_ref[...] = tail[:, SSD_HEADS:SSD_HEADS + tn].astype(o_ref.dtype)


def _in_proj(x, g, w_t, q_scale):
    t = x.shape[0]
    tm, tn = PROJ_TM, PROJ_TN
    assert w_t.shape[0] == PROJ_MAIN_TILES * tn + SSD_HEADS + tn
    return pl.pallas_call(
        functools.partial(_in_proj_body, q_scale),
        grid=(PROJ_MAIN_TILES + 1, t // tm),
        in_specs=[
            pl.BlockSpec((tm, D_MODEL), lambda j, i: (i, 0)),
            pl.BlockSpec((1, D_MODEL), lambda j, i: (0, 0)),
            pl.BlockSpec((tn, D_MODEL), lambda j, i: (jnp.minimum(j, PROJ_MAIN_TILES - 1), 0)),
            pl.BlockSpec((pl.Element(SSD_HEADS + tn), pl.Element(D_MODEL)),
                         lambda j, i: (PROJ_MAIN_TILES * tn, 0), pipeline_mode=pl.Buffered(1)),
        ],
        out_specs=[
            pl.BlockSpec((tm, tn), lambda j, i: (i, j)),
            pl.BlockSpec((tm, LANES), lambda j, i: (jnp.where(j == PROJ_MAIN_TILES, i, 0), 0)),
        ],
        out_shape=[
            jax.ShapeDtypeStruct((t, PROJ_WIDTH), BF16),
            jax.ShapeDtypeStruct((t, LANES), F32),
        ],
        compiler_params=pltpu.CompilerParams(
            dimension_semantics=("arbitrary", "arbitrary"), vmem_limit_bytes=VMEM_LIMIT),
        name="in_proj",
    )(x, g, w_t, w_t)


def _attn_body(lam_init, seq, q_ref, k_ref, v_ref, lam_ref, g_ref, o_ref,
               qq_ref, m_ref, acc_ref, vx_ref, mask_ref):
    tq = ATTN_TQ
    heads = range(ATTN_HPS)
    lamv = lam_ref[...]
    lam = (jnp.exp(jnp.sum(lamv[0:1] * lamv[1:2], axis=-1, keepdims=True))
           - jnp.exp(jnp.sum(lamv[2:3] * lamv[3:4], axis=-1, keepdims=True)) + lam_init)
    lane = lax.broadcasted_iota(jnp.int32, (1, DA_PAIR), 1)
    first_half = lane < DA_HEAD_DIM
    ones_col = jnp.broadcast_to(jnp.where(lane == 0, 1.0, 0.0), (seq, DA_PAIR)).astype(BF16)
    for hh in heads:
        vx_ref[hh, :, :DA_PAIR] = v_ref[:, hh * DA_PAIR:(hh + 1) * DA_PAIR]
        vx_ref[hh, :, DA_PAIR:] = ones_col

    row = lax.broadcasted_iota(jnp.int32, (2 * tq, tq), 0)
    col = lax.broadcasted_iota(jnp.int32, (2 * tq, tq), 1)
    mask_ref[...] = jnp.where(jnp.where(row >= tq, row - tq, row) >= col, 0.0, -jnp.inf)

    def kv_block(slot, k0, width, first):
        def slab(hh, c):
            s = _dot_nt(qq_ref[hh], k_ref[pl.ds(pl.multiple_of(k0 + c, ATTN_KSLAB), ATTN_KSLAB),
                                          hh * DA_PAIR:(hh + 1) * DA_PAIR])
            d = c - (width - tq)
            return s + mask_ref[:, d:d + ATTN_KSLAB] if first and d >= 0 else s

        scores = [jnp.concatenate([slab(hh, c) for c in range(0, width, ATTN_KSLAB)], axis=1)
                  for hh in heads]
        for hh, s in zip(heads, scores):
            m_new = jnp.broadcast_to(jnp.max(s, axis=-1, keepdims=True), (2 * tq, LANES))
            if not first:
                m_old = m_ref[slot, hh]
                m_new = jnp.maximum(m_old, m_new)
                alpha = jnp.exp2(m_old - m_new)
                alpha2 = jnp.concatenate([alpha, alpha], axis=1)
            p = jnp.concatenate(
                [jnp.exp2(s[:, c:c + LANES] - m_new).astype(BF16) for c in range(0, width, LANES)],
                axis=1)
            vb = vx_ref[hh, pl.ds(k0, width), :]
            for r in (0, tq):
                pv = _dot(p[r:r + tq], vb)
                if first:
                    acc_ref[slot, hh, r:r + tq, :] = pv
                else:
                    acc_ref[slot, hh, r:r + tq, :] = (
                        acc_ref[slot, hh, r:r + tq, :] * alpha2[r:r + tq] + pv)
            m_ref[slot, hh] = m_new

    def load_queries(r0):
        for hh in heads:
            q = q_ref[pl.ds(r0, tq), hh * DA_PAIR:(hh + 1) * DA_PAIR]
            qq_ref[hh, 0:tq, :] = jnp.where(first_half, q, jnp.zeros_like(q))
            qq_ref[hh, tq:2 * tq, :] = jnp.where(first_half, jnp.zeros_like(q), q)

    def finish(slot, r0):
        for hh in heads:
            acc = acc_ref[slot, hh]
            o = acc[:, :DA_PAIR] / acc[:, DA_PAIR:DA_PAIR + 1]
            res = o[:tq] - lam * o[tq:]
            out = _rms(res, g_ref[...], SUBLN_EPS) * (1.0 - lam_init)
            o_ref[pl.ds(r0, tq), hh * DA_PAIR:(hh + 1) * DA_PAIR] = out.astype(o_ref.dtype)

    def earlier_keys(slot, a):
        def double_step(jj, c):
            kv_block(slot, pl.multiple_of(jj * (2 * tq), 2 * tq), 2 * tq, False)
            return c

        lax.fori_loop(0, a, double_step, 0)

    def tile_pair(a, finish_previous):
        r_even = pl.multiple_of(a * (2 * tq), 2 * tq)
        r_odd = pl.multiple_of(r_even + tq, tq)
        load_queries(r_even)
        kv_block(0, r_even, tq, True)
        if finish_previous:
            finish(1, pl.multiple_of(r_even - tq, tq))
        earlier_keys(0, a)
        load_queries(r_odd)
        kv_block(1, r_even, 2 * tq, True)
        finish(0, r_even)
        earlier_keys(1, a)

    def later_pair(a, carry):
        tile_pair(a, True)
        return carry

    tile_pair(0, False)
    lax.fori_loop(1, seq // (2 * tq), later_pair, 0)
    finish(1, seq - tq)


def _diff_attn(proj, lam_params, subln_g, lam_init, batch, seq):
    t = batch * seq
    tq = ATTN_TQ
    w = ATTN_HPS * DA_PAIR
    steps = DA_HEADS // ATTN_HPS
    return pl.pallas_call(
        functools.partial(_attn_body, lam_init, seq),
        grid=(batch, steps),
        in_specs=[
            pl.BlockSpec((seq, w), lambda b, h: (b, h)),
            pl.BlockSpec((seq, w), lambda b, h: (b, steps + h)),
            pl.BlockSpec((seq, w), lambda b, h: (b, 2 * steps + h)),
            pl.BlockSpec((4, DA_HEAD_DIM), lambda b, h: (0, 0)),
            pl.BlockSpec((1, DA_PAIR), lambda b, h: (0, 0)),
        ],
        out_specs=pl.BlockSpec((seq, w), lambda b, h: (b, h)),
        out_shape=jax.ShapeDtypeStruct((t, DA_V), BF16),
        scratch_shapes=[
            pltpu.VMEM((ATTN_HPS, 2 * tq, DA_PAIR), BF16),
            pltpu.VMEM((2, ATTN_HPS, 2 * tq, LANES), F32),
            pltpu.VMEM((2, ATTN_HPS, 2 * tq, 2 * DA_PAIR), F32),
            pltpu.VMEM((ATTN_HPS, seq, 2 * DA_PAIR), BF16),
            pltpu.VMEM((2 * tq, tq), F32),
        ],
        compiler_params=pltpu.CompilerParams(
            dimension_semantics=("parallel", "parallel"), vmem_limit_bytes=VMEM_LIMIT),
        name="diff_attn",
    )(proj, proj, proj, lam_params, subln_g)


def _ssd_body(*refs):
    nz, nx = SSD_INNER // PROJ_PIECE, SSD_CONV_DIM // PROJ_PIECE
    z_refs, xbc_refs = refs[:nz], refs[nz:nz + nx]
    rest = refs[nz + nx:]
    xe_ref, rt_ref = rest[-3], rest[-2]

    @pl.when(pl.program_id(1) == 0)
    def _():
        xe_ref[0:SSD_HIST, :] = jnp.zeros((SSD_HIST, SSD_CONV_DIM), BF16)
        rt_ref[...] = jnp.zeros(rt_ref.shape, F32)

    for sub in range(SSD_SUB):
        _ssd_chunk(pl.ds(sub * SSD_CHUNK, SSD_CHUNK), z_refs, xbc_refs, *rest)


def _ssd_chunk(rows, z_refs, xbc_refs, dt_ref, shift_ref, cw_ref, cb_ref, dtb_ref, alog_ref,
               dexp_ref, ng_ref, y_ref, xe_ref, rt_ref, ybuf_ref):
    lc = SSD_CHUNK
    hist = SSD_HIST

    xb = jnp.concatenate([r[rows, :] for r in xbc_refs], axis=1)
    xe_ref[hist:hist + lc, :] = xb
    xe = xe_ref[...]
    xe_ref[0:hist, :] = xb[lc - hist:lc, :]
    cw = cw_ref[...].astype(BF16)
    taps = jnp.concatenate([xe * cw[k:k + 1, :] for k in range(SSD_CONV)], axis=0)
    act = _silu(_dot(shift_ref[...], taps) + cb_ref[...])

    dt = jax.nn.softplus(dt_ref[rows, :] + dtb_ref[...])
    adt = -jnp.exp(alog_ref[...]) * dt
    ri = lax.broadcasted_iota(jnp.int32, (lc, lc), 0)
    ci = lax.broadcasted_iota(jnp.int32, (lc, lc), 1)
    lower = ri >= ci
    tril = jnp.where(lower, 1.0, 0.0)
    acs = jnp.dot(tril, adt, precision=lax.Precision.HIGHEST, preferred_element_type=F32)
    acs_t = acs.T
    dt_t = dt.T
    tot_t = acs_t[:, lc - 1:lc]
    w_t = jnp.exp(tot_t - acs_t) * dt_t
    etot_t = jnp.exp(tot_t)
    src_t = acs_t - jnp.log(dt_t)

    lane = lax.broadcasted_iota(jnp.int32, (lc, LANES), 1)
    left = lane < SSD_HEAD_DIM
    zero16 = jnp.zeros((lc, LANES), BF16)

    for g in range(SSD_GROUPS):
        b0 = SSD_INNER + g * SSD_STATE
        c0 = SSD_INNER + SSD_GROUPS * SSD_STATE + g * SSD_STATE
        cg = act[:, c0:c0 + SSD_STATE]
        bg_t = act[:, b0:b0 + SSD_STATE].T
        cg16 = cg.astype(BF16)
        bgt16 = bg_t.astype(BF16)
        cbg16 = _dot(cg16, bgt16).astype(BF16)
        for pr in range(SSD_HEADS_PER_GROUP // 2):
            h0 = g * SSD_HEADS_PER_GROUP + 2 * pr
            x0 = h0 * SSD_HEAD_DIM
            xpair = act[:, x0:x0 + LANES]
            rtpair = rt_ref[:, x0:x0 + LANES]
            x16 = xpair.astype(BF16)
            r16 = rtpair.astype(BF16)
            xl, xr = jnp.where(left, x16, zero16), jnp.where(left, zero16, x16)
            rl, rr = jnp.where(left, r16, zero16), jnp.where(left, zero16, r16)
            mh, ech, wh = [], [], []
            for h in (h0, h0 + 1):
                bc = jnp.broadcast_to(acs[:, h:h + 1], (lc, lc))
                seg = jnp.where(lower, bc - src_t[h:h + 1, :], -jnp.inf)
                mh.append(cbg16 * jnp.exp(seg).astype(BF16))
                ech.append(cg16 * jnp.exp(bc).astype(BF16))
                wh.append(bgt16 * jnp.broadcast_to(w_t[h:h + 1, :], (SSD_STATE, lc)).astype(BF16))
            ypair = _dot(jnp.concatenate(mh + ech, axis=1),
                         jnp.concatenate([xl, xr, rl, rr], axis=0))
            spair = _dot(jnp.concatenate(wh, axis=1), jnp.concatenate([xl, xr], axis=0))
            e0 = jnp.broadcast_to(etot_t[h0:h0 + 1, :], (1, LANES))
            e1 = jnp.broadcast_to(etot_t[h0 + 1:h0 + 2, :], (1, LANES))
            rt_ref[:, x0:x0 + LANES] = rtpair * jnp.where(left[0:1], e0, e1) + spair
            yp = ypair + dexp_ref[:, x0:x0 + LANES] * xpair
            zc = x0 % PROJ_PIECE
            zpair = z_refs[x0 // PROJ_PIECE][rows, zc:zc + LANES].astype(F32)
            ybuf_ref[:, x0:x0 + LANES] = yp * _silu(zpair)

    for g in range(SSD_GROUPS):
        g0 = g * SSD_GROUP_W
        yg = ybuf_ref[:, g0:g0 + SSD_GROUP_W]
        y_ref[rows, g0:g0 + SSD_GROUP_W] = _rms(
            yg, ng_ref[:, g0:g0 + SSD_GROUP_W], SUBLN_EPS).astype(y_ref.dtype)


def _ssd(proj, dt_raw, conv_w, conv_b, dt_bias, a_log, d_exp, norm_g, batch, seq):
    t = batch * seq
    lc = SSD_CHUNK
    tr = SSD_SUB * lc
    nc = seq // tr
    row = lambda b, c: b * nc + c
    rows_e = SSD_HIST + lc
    col = jnp.arange(SSD_CONV * rows_e)
    src = SSD_HIST + jnp.arange(lc)[:, None] - (SSD_CONV - 1) + col[None, :] // rows_e
    shift = (col[None, :] % rows_e == src).astype(BF16)

    def pieces(off, width):
        return [pl.BlockSpec((tr, PROJ_PIECE), functools.partial(
            lambda b, c, blk: (row(b, c), blk), blk=(off + p0) // PROJ_PIECE))
            for p0 in range(0, width, PROJ_PIECE)]

    n_pieces = (SSD_INNER + SSD_CONV_DIM) // PROJ_PIECE
    return pl.pallas_call(
        _ssd_body,
        grid=(batch, nc),
        in_specs=pieces(PROJ_Z_OFF, SSD_INNER) + pieces(PROJ_XBC_OFF, SSD_CONV_DIM) + [
            pl.BlockSpec((tr, LANES), lambda b, c: (row(b, c), 0)),
            pl.BlockSpec((lc, SSD_CONV * rows_e), lambda b, c: (0, 0)),
            pl.BlockSpec((SSD_CONV, SSD_CONV_DIM), lambda b, c: (0, 0)),
            pl.BlockSpec((1, SSD_CONV_DIM), lambda b, c: (0, 0)),
            pl.BlockSpec((1, LANES), lambda b, c: (0, 0)),
            pl.BlockSpec((1, LANES), lambda b, c: (0, 0)),
            pl.BlockSpec((1, SSD_INNER), lambda b, c: (0, 0)),
            pl.BlockSpec((1, SSD_INNER), lambda b, c: (0, 0)),
        ],
        out_specs=pl.BlockSpec((tr, SSD_INNER), lambda b, c: (row(b, c), 0)),
        out_shape=jax.ShapeDtypeStruct((t, SSD_INNER), BF16),
        scratch_shapes=[
            pltpu.VMEM((SSD_HIST + lc, SSD_CONV_DIM), BF16),
            pltpu.VMEM((SSD_STATE, SSD_INNER), F32),
            pltpu.VMEM((lc, SSD_INNER), F32),
        ],
        compiler_params=pltpu.CompilerParams(
            dimension_semantics=("parallel", "arbitrary"), vmem_limit_bytes=VMEM_LIMIT),
        name="ssd",
    )(*([proj] * n_pieces), dt_raw, shift, conv_w, conv_b, dt_bias, a_log, d_exp, norm_g)


def _mix_xattn_body(ao_ref, ys_ref, gl_ref, bg_ref, x_ref, wa_ref, ws_ref, wm_ref, mpost_ref,
                    xpre_ref, wq_ref, kv_ref, wo_ref, xpost_ref, o_ref):
    attn_out = _dot(ao_ref[...], wa_ref[...])
    ssd_out = _dot(ys_ref[...], ws_ref[...])
    gates = jax.nn.sigmoid(gl_ref[...].astype(F32) + bg_ref[...])
    mixed = gates[:, :D_MODEL] * attn_out + gates[:, D_MODEL:] * ssd_out
    mixed = _dot(mixed, wm_ref[...])
    x = x_ref[...] + _rms(mixed, mpost_ref[...], NORM_EPS)

    hq = _rms(x, xpre_ref[...], NORM_EPS)
    qx = (_dot(hq, wq_ref[...]) * (XA_HEAD_DIM ** -0.5)).astype(BF16)
    heads = []
    for h in range(XA_HEADS):
        c0 = h * XA_HEAD_DIM
        s = _dot_nt(qx[:, c0:c0 + XA_HEAD_DIM], kv_ref[:, c0:c0 + XA_HEAD_DIM])
        e = jnp.exp(s - jnp.max(s, axis=-1, keepdims=True))
        denom = jnp.sum(e, axis=-1, keepdims=True)
        oh = _dot(e.astype(BF16), kv_ref[:, D_MODEL + c0:D_MODEL + c0 + XA_HEAD_DIM])
        heads.append(oh / denom)
    xo = _dot(jnp.concatenate(heads, axis=1), wo_ref[...])
    o_ref[...] = x + _rms(xo, xpost_ref[...], NORM_EPS)


def _mix_xattn(attn_o, y_ssd, proj, b_gate, x, w_attn, w_ssd, w_mix, mix_post_g,
               xa_pre_g, w_q, kv, w_o, xa_post_g, seq):
    t = x.shape[0]
    tm = MIX_TM
    gw = N_BRANCH * D_MODEL
    per_batch = seq // tm
    row_tile = lambda w: pl.BlockSpec((tm, w), lambda i: (i, 0))
    return pl.pallas_call(
        _mix_xattn_body,
        grid=(t // tm,),
        in_specs=[
            row_tile(DA_V),
            row_tile(SSD_INNER),
            pl.BlockSpec((tm, gw), lambda i: (i, PROJ_GATE_OFF // gw)),
            _const_spec((1, gw)),
            row_tile(D_MODEL),
            _const_spec((DA_V, D_MODEL)),
            _const_spec((SSD_INNER, D_MODEL)),
            _const_spec((D_MODEL, D_MODEL)),
            _const_spec((1, D_MODEL)),
            _const_spec((1, D_MODEL)),
            _const_spec((D_MODEL, D_MODEL)),
            pl.BlockSpec((MEM_LEN, 2 * D_MODEL), lambda i: (i // per_batch, 0)),
            _const_spec((D_MODEL, D_MODEL)),
            _const_spec((1, D_MODEL)),
        ],
        out_specs=row_tile(D_MODEL),
        out_shape=jax.ShapeDtypeStruct((t, D_MODEL), F32),
        compiler_params=pltpu.CompilerParams(
            dimension_semantics=("parallel",), vmem_limit_bytes=VMEM_LIMIT),
        name="mix_xattn",
    )(attn_o, y_ssd, proj, b_gate, x, w_attn, w_ssd, w_mix, mix_post_g,
      xa_pre_g, w_q, kv, w_o, xa_post_g)


def _row(v):
    return v.reshape(1, -1)


def _pad_lanes(v):
    return jnp.pad(v, ((0, 0), (0, LANES - v.shape[1])))


def _layer(x, mem, layer_idx, batch, seq,
           ffn1_pre_g, ffn1_post_g, ffn1_w_gu, ffn1_w_down,
           mix_pre_g, mix_post_g, w_in, b_gate,
           da_lambda_q1, da_lambda_k1, da_lambda_q2, da_lambda_k2, da_subln_g,
           ssd_conv_w, ssd_conv_b, ssd_dt_bias, ssd_A_log, ssd_D, ssd_norm_g,
           w_branch_attn, w_branch_ssd, w_mix_out,
           xa_pre_g, xa_post_g, mem_norm_g, xa_w_q, xa_w_kv, xa_w_o,
           ffn2_pre_g, ffn2_post_g, ffn2_w_gu, ffn2_w_down):
    x = _ffn(x, _row(ffn1_pre_g), ffn1_w_gu, ffn1_w_down, _row(ffn1_post_g))

    q_scale = DA_HEAD_DIM ** -0.5 * math.log2(math.e)
    proj, dt_raw = _in_proj(x, _row(mix_pre_g), w_in.T, q_scale)

    lam_init = 0.8 - 0.6 * math.exp(-0.3 * layer_idx)
    lam_params = jnp.stack([da_lambda_q1, da_lambda_k1, da_lambda_q2, da_lambda_k2])
    attn_o = _diff_attn(proj, lam_params, _row(da_subln_g), lam_init, batch, seq)

    y_ssd = _ssd(proj, dt_raw, ssd_conv_w, _row(ssd_conv_b), _pad_lanes(_row(ssd_dt_bias)),
                 _pad_lanes(_row(ssd_A_log)), _row(jnp.repeat(ssd_D, SSD_HEAD_DIM)),
                 _row(ssd_norm_g), batch, seq)

    mem2 = mem.reshape(batch * MEM_LEN, D_MODEL)
    kv = _norm_proj(mem2, _row(mem_norm_g), xa_w_kv, MEM_LEN)
    x = _mix_xattn(attn_o, y_ssd, proj, _row(b_gate), x, w_branch_attn, w_branch_ssd, w_mix_out,
                   _row(mix_post_g), _row(xa_pre_g), xa_w_q, kv, xa_w_o, _row(xa_post_g), seq)

    x = _ffn(x, _row(ffn2_pre_g), ffn2_w_gu, ffn2_w_down, _row(ffn2_post_g))
    return x


def kernel(x, mem, ffn1_pre_g, ffn1_post_g, ffn1_w_gu, ffn1_w_down, mix_pre_g, mix_post_g, w_in, b_gate, da_lambda_q1, da_lambda_k1, da_lambda_q2, da_lambda_k2, da_subln_g, ssd_conv_w, ssd_conv_b, ssd_dt_bias, ssd_A_log, ssd_D, ssd_norm_g, w_branch_attn, w_branch_ssd, w_mix_out, xa_pre_g, xa_post_g, mem_norm_g, xa_w_q, xa_w_kv, xa_w_o, ffn2_pre_g, ffn2_post_g, ffn2_w_gu, ffn2_w_down):
    batch, seq, d = x.shape
    params = (ffn1_pre_g, ffn1_post_g, ffn1_w_gu, ffn1_w_down, mix_pre_g, mix_post_g, w_in, b_gate,
              da_lambda_q1, da_lambda_k1, da_lambda_q2, da_lambda_k2, da_subln_g,
              ssd_conv_w, ssd_conv_b, ssd_dt_bias, ssd_A_log, ssd_D, ssd_norm_g,
              w_branch_attn, w_branch_ssd, w_mix_out,
              xa_pre_g, xa_post_g, mem_norm_g, xa_w_q, xa_w_kv, xa_w_o,
              ffn2_pre_g, ffn2_post_g, ffn2_w_gu, ffn2_w_down)
    h = x.reshape(batch * seq, d)
    for layer in range(ffn1_pre_g.shape[0]):
        h = _layer(h, mem, layer, batch, seq, *[p[layer] for p in params])
    return h.reshape(batch, seq, d)
```

```python
import functools
import math

import jax
import jax.numpy as jnp
from jax import lax
from jax.experimental import pallas as pl
from jax.experimental.pallas import tpu as pltpu

F32 = jnp.float32
BF16 = jnp.bfloat16

D_MODEL = 1024
MEM_LEN = 256
DA_HEADS = 8
DA_HEAD_DIM = 64
DA_PAIR = 2 * DA_HEAD_DIM
DA_QK = DA_HEADS * DA_PAIR
DA_V = DA_HEADS * DA_PAIR
SSD_INNER = 2 * D_MODEL
SSD_HEAD_DIM = 64
SSD_HEADS = SSD_INNER // SSD_HEAD_DIM
SSD_GROUPS = 4
SSD_HEADS_PER_GROUP = SSD_HEADS // SSD_GROUPS
SSD_STATE = 128
SSD_CONV = 4
SSD_CHUNK = 128
SSD_GROUP_W = SSD_INNER // SSD_GROUPS
SSD_SUB = 4
SSD_HIST = 16
SSD_CONV_DIM = SSD_INNER + 2 * SSD_GROUPS * SSD_STATE
XA_HEADS = 4
XA_HEAD_DIM = D_MODEL // XA_HEADS
D_FF = 2816
N_BRANCH = 2
NORM_EPS = 1e-6
SUBLN_EPS = 1e-5

LANES = 128
VMEM_LIMIT = 56 * 1024 * 1024

PROJ_Z_OFF = 3 * DA_QK
PROJ_XBC_OFF = PROJ_Z_OFF + SSD_INNER
PROJ_GATE_OFF = PROJ_XBC_OFF + SSD_CONV_DIM
PROJ_WIDTH = PROJ_GATE_OFF + N_BRANCH * D_MODEL
PROJ_PIECE = 1024

FFN_TM = 512
FFN_CHUNK = 256
PROJ_TM = 1024
PROJ_TN = 2048
PROJ_MAIN_TILES = PROJ_GATE_OFF // PROJ_TN
ATTN_TQ = 512
ATTN_KSLAB = 256
ATTN_HPS = 2
MIX_TM = 512


def _rms(x, g, eps):
    return x * lax.rsqrt(jnp.mean(x * x, axis=-1, keepdims=True) + eps) * g


def _silu(x):
    h = 0.5 * x
    return h + h * jnp.tanh(h)


def _dot(a, b):
    return jnp.dot(a, b, preferred_element_type=F32)


def _dot_nt(a, b):
    return lax.dot_general(a, b, (((1,), (1,)), ((), ())), preferred_element_type=F32)


def _const_spec(shape):
    nd = len(shape)
    return pl.BlockSpec(shape, lambda *_: (0,) * nd, pipeline_mode=pl.Buffered(1))


def _ffn_body(x_ref, pre_ref, wgu_ref, wd_ref, post_ref, o_ref):
    x = x_ref[...]
    hn = _rms(x, pre_ref[...], NORM_EPS)
    acc = None
    for c0 in range(0, D_FF, FFN_CHUNK):
        g = _dot(hn, wgu_ref[:, c0:c0 + FFN_CHUNK])
        u = _dot(hn, wgu_ref[:, D_FF + c0:D_FF + c0 + FFN_CHUNK])
        a = _silu(g) * u
        d = _dot(a, wd_ref[c0:c0 + FFN_CHUNK, :])
        acc = d if acc is None else acc + d
    o_ref[...] = x + 0.5 * _rms(acc, post_ref[...], NORM_EPS)


def _ffn(x, pre_g, w_gu, w_down, post_g):
    t = x.shape[0]
    return pl.pallas_call(
        _ffn_body,
        grid=(t // FFN_TM,),
        in_specs=[
            pl.BlockSpec((FFN_TM, D_MODEL), lambda i: (i, 0)),
            _const_spec((1, D_MODEL)),
            _const_spec((D_MODEL, 2 * D_FF)),
            _const_spec((D_FF, D_MODEL)),
            _const_spec((1, D_MODEL)),
        ],
        out_specs=pl.BlockSpec((FFN_TM, D_MODEL), lambda i: (i, 0)),
        out_shape=jax.ShapeDtypeStruct((t, D_MODEL), F32),
        compiler_params=pltpu.CompilerParams(
            dimension_semantics=("parallel",), vmem_limit_bytes=VMEM_LIMIT),
        name="ffn",
    )(x, pre_g, w_gu, w_down, post_g)


def _norm_proj_body(x_ref, g_ref, w_ref, o_ref):
    hn = _rms(x_ref[...], g_ref[...], NORM_EPS)
    o_ref[...] = _dot(hn, w_ref[...]).astype(o_ref.dtype)


def _norm_proj(x, g, w, tm):
    t, n = x.shape[0], w.shape[1]
    return pl.pallas_call(
        _norm_proj_body,
        grid=(t // tm,),
        in_specs=[
            pl.BlockSpec((tm, D_MODEL), lambda i: (i, 0)),
            _const_spec((1, D_MODEL)),
            _const_spec((D_MODEL, n)),
        ],
        out_specs=pl.BlockSpec((tm, n), lambda i: (i, 0)),
        out_shape=jax.ShapeDtypeStruct((t, n), BF16),
        compiler_params=pltpu.CompilerParams(
            dimension_semantics=("parallel",), vmem_limit_bytes=VMEM_LIMIT),
        name="norm_proj",
    )(x, g, w)


def _in_proj_body(q_scale, x_ref, g_ref, w_ref, wt_ref, o_ref, dt_ref):
    j = pl.program_id(0)
    tn = o_ref.shape[1]

    @pl.when(j < PROJ_MAIN_TILES)
    def _():
        hn = _rms(x_ref[...], g_ref[...], NORM_EPS)
        col = j * tn + lax.broadcasted_iota(jnp.int32, (1, tn), 1)
        scale = jnp.where(col < DA_QK, q_scale, 1.0)
        o_ref[...] = (_dot_nt(hn, w_ref[...]) * scale).astype(o_ref.dtype)

    @pl.when(j == PROJ_MAIN_TILES)
    def _():
        hn = _rms(x_ref[...], g_ref[...], NORM_EPS)
        tail = _dot_nt(hn, wt_ref[...])
        dt_ref[...] = tail[:, :LANES]
        o_ref[...] = tail[:, SSD_HEADS:SSD_HEADS + tn].astype(o_ref.dtype)


def _in_proj(x, g, w_t, q_scale):
    t = x.shape[0]
    tm, tn = PROJ_TM, PROJ_TN
    assert w_t.shape[0] == PROJ_MAIN_TILES * tn + SSD_HEADS + tn
    return pl.pallas_call(
        functools.partial(_in_proj_body, q_scale),
        grid=(PROJ_MAIN_TILES + 1, t // tm),
        in_specs=[
            pl.BlockSpec((tm, D_MODEL), lambda j, i: (i, 0)),
            pl.BlockSpec((1, D_MODEL), lambda j, i: (0, 0)),
            pl.BlockSpec((tn, D_MODEL), lambda j, i: (jnp.minimum(j, PROJ_MAIN_TILES - 1), 0)),
            pl.BlockSpec((pl.Element(SSD_HEADS + tn), pl.Element(D_MODEL)),
                         lambda j, i: (PROJ_MAIN_TILES * tn, 0), pipeline_mode=pl.Buffered(1)),
        ],
        out_specs=[
            pl.BlockSpec((tm, tn), lambda j, i: (i, j)),
            pl.BlockSpec((tm, LANES), lambda j, i: (jnp.where(j == PROJ_MAIN_TILES, i, 0), 0)),
        ],
        out_shape=[
            jax.ShapeDtypeStruct((t, PROJ_WIDTH), BF16),
            jax.ShapeDtypeStruct((t, LANES), F32),
        ],
        compiler_params=pltpu.CompilerParams(
            dimension_semantics=("arbitrary", "arbitrary"), vmem_limit_bytes=VMEM_LIMIT),
        name="in_proj",
    )(x, g, w_t, w_t)


def _attn_body(lam_init, seq, q_ref, k_ref, v_ref, lam_ref, g_ref, o_ref,
               qq_ref, m_ref, acc_ref, vx_ref, mask_ref):
    tq = ATTN_TQ
    heads = range(ATTN_HPS)
    lamv = lam_ref[...]
    lam = (jnp.exp(jnp.sum(lamv[0:1] * lamv[1:2], axis=-1, keepdims=True))
           - jnp.exp(jnp.sum(lamv[2:3] * lamv[3:4], axis=-1, keepdims=True)) + lam_init)
    lane = lax.broadcasted_iota(jnp.int32, (1, DA_PAIR), 1)
    first_half = lane < DA_HEAD_DIM
    ones_col = jnp.broadcast_to(jnp.where(lane == 0, 1.0, 0.0), (seq, DA_PAIR)).astype(BF16)
    for hh in heads:
        vx_ref[hh, :, :DA_PAIR] = v_ref[:, hh * DA_PAIR:(hh + 1) * DA_PAIR]
        vx_ref[hh, :, DA_PAIR:] = ones_col

    row = lax.broadcasted_iota(jnp.int32, (2 * tq, tq), 0)
    col = lax.broadcasted_iota(jnp.int32, (2 * tq, tq), 1)
    mask_ref[...] = jnp.where(jnp.where(row >= tq, row - tq, row) >= col, 0.0, -jnp.inf)

    def kv_block(slot, k0, width, first):
        def slab(hh, c):
            s = _dot_nt(qq_ref[hh], k_ref[pl.ds(pl.multiple_of(k0 + c, ATTN_KSLAB), ATTN_KSLAB),
                                          hh * DA_PAIR:(hh + 1) * DA_PAIR])
            d = c - (width - tq)
            return s + mask_ref[:, d:d + ATTN_KSLAB] if first and d >= 0 else s

        scores = [jnp.concatenate([slab(hh, c) for c in range(0, width, ATTN_KSLAB)], axis=1)
                  for hh in heads]
        for hh, s in zip(heads, scores):
            m_new = jnp.broadcast_to(jnp.max(s, axis=-1, keepdims=True), (2 * tq, LANES))
            if not first:
                m_old = m_ref[slot, hh]
                m_new = jnp.maximum(m_old, m_new)
                alpha = jnp.exp2(m_old - m_new)
                alpha2 = jnp.concatenate([alpha, alpha], axis=1)
            p = jnp.concatenate(
                [jnp.exp2(s[:, c:c + LANES] - m_new).astype(BF16) for c in range(0, width, LANES)],
                axis=1)
            vb = vx_ref[hh, pl.ds(k0, width), :]
            for r in (0, tq):
                pv = _dot(p[r:r + tq], vb)
                if first:
                    acc_ref[slot, hh, r:r + tq, :] = pv
                else:
                    acc_ref[slot, hh, r:r + tq, :] = (
                        acc_ref[slot, hh, r:r + tq, :] * alpha2[r:r + tq] + pv)
            m_ref[slot, hh] = m_new

    def load_queries(r0):
        for hh in heads:
            q = q_ref[pl.ds(r0, tq), hh * DA_PAIR:(hh + 1) * DA_PAIR]
            qq_ref[hh, 0:tq, :] = jnp.where(first_half, q, jnp.zeros_like(q))
            qq_ref[hh, tq:2 * tq, :] = jnp.where(first_half, jnp.zeros_like(q), q)

    def finish(slot, r0):
        for hh in heads:
            acc = acc_ref[slot, hh]
            o = acc[:, :DA_PAIR] / acc[:, DA_PAIR:DA_PAIR + 1]
            res = o[:tq] - lam * o[tq:]
            out = _rms(res, g_ref[...], SUBLN_EPS) * (1.0 - lam_init)
            o_ref[pl.ds(r0, tq), hh * DA_PAIR:(hh + 1) * DA_PAIR] = out.astype(o_ref.dtype)

    def earlier_keys(slot, a):
        def double_step(jj, c):
            kv_block(slot, pl.multiple_of(jj * (2 * tq), 2 * tq), 2 * tq, False)
            return c

        lax.fori_loop(0, a, double_step, 0)

    def tile_pair(a, finish_previous):
        r_even = pl.multiple_of(a * (2 * tq), 2 * tq)
        r_odd = pl.multiple_of(r_even + tq, tq)
        load_queries(r_even)
        kv_block(0, r_even, tq, True)
        if finish_previous:
            finish(1, pl.multiple_of(r_even - tq, tq))
        earlier_keys(0, a)
        load_queries(r_odd)
        kv_block(1, r_even, 2 * tq, True)
        finish(0, r_even)
        earlier_keys(1, a)

    def later_pair(a, carry):
        tile_pair(a, True)
        return carry

    tile_pair(0, False)
    lax.fori_loop(1, seq // (2 * tq), later_pair, 0)
    finish(1, seq - tq)


def _diff_attn(proj, lam_params, subln_g, lam_init, batch, seq):
    t = batch * seq
    tq = ATTN_TQ
    w = ATTN_HPS * DA_PAIR
    steps = DA_HEADS // ATTN_HPS
    return pl.pallas_call(
        functools.partial(_attn_body, lam_init, seq),
        grid=(batch, steps),
        in_specs=[
            pl.BlockSpec((seq, w), lambda b, h: (b, h)),
            pl.BlockSpec((seq, w), lambda b, h: (b, steps + h)),
            pl.BlockSpec((seq, w), lambda b, h: (b, 2 * steps + h)),
            pl.BlockSpec((4, DA_HEAD_DIM), lambda b, h: (0, 0)),
            pl.BlockSpec((1, DA_PAIR), lambda b, h: (0, 0)),
        ],
        out_specs=pl.BlockSpec((seq, w), lambda b, h: (b, h)),
        out_shape=jax.ShapeDtypeStruct((t, DA_V), BF16),
        scratch_shapes=[
            pltpu.VMEM((ATTN_HPS, 2 * tq, DA_PAIR), BF16),
            pltpu.VMEM((2, ATTN_HPS, 2 * tq, LANES), F32),
            pltpu.VMEM((2, ATTN_HPS, 2 * tq, 2 * DA_PAIR), F32),
            pltpu.VMEM((ATTN_HPS, seq, 2 * DA_PAIR), BF16),
            pltpu.VMEM((2 * tq, tq), F32),
        ],
        compiler_params=pltpu.CompilerParams(
            dimension_semantics=("parallel", "parallel"), vmem_limit_bytes=VMEM_LIMIT),
        name="diff_attn",
    )(proj, proj, proj, lam_params, subln_g)


def _ssd_body(*refs):
    nz, nx = SSD_INNER // PROJ_PIECE, SSD_CONV_DIM // PROJ_PIECE
    z_refs, xbc_refs = refs[:nz], refs[nz:nz + nx]
    rest = refs[nz + nx:]
    xe_ref, rt_ref = rest[-3], rest[-2]

    @pl.when(pl.program_id(1) == 0)
    def _():
        xe_ref[0:SSD_HIST, :] = jnp.zeros((SSD_HIST, SSD_CONV_DIM), BF16)
        rt_ref[...] = jnp.zeros(rt_ref.shape, F32)

    for sub in range(SSD_SUB):
        _ssd_chunk(pl.ds(sub * SSD_CHUNK, SSD_CHUNK), z_refs, xbc_refs, *rest)


def _ssd_chunk(rows, z_refs, xbc_refs, dt_ref, shift_ref, cw_ref, cb_ref, dtb_ref, alog_ref,
               dexp_ref, ng_ref, y_ref, xe_ref, rt_ref, ybuf_ref):
    lc = SSD_CHUNK
    hist = SSD_HIST

    xb = jnp.concatenate([r[rows, :] for r in xbc_refs], axis=1)
    xe_ref[hist:hist + lc, :] = xb
    xe = xe_ref[...]
    xe_ref[0:hist, :] = xb[lc - hist:lc, :]
    cw = cw_ref[...].astype(BF16)
    taps = jnp.concatenate([xe * cw[k:k + 1, :] for k in range(SSD_CONV)], axis=0)
    act = _silu(_dot(shift_ref[...], taps) + cb_ref[...])

    dt = jax.nn.softplus(dt_ref[rows, :] + dtb_ref[...])
    adt = -jnp.exp(alog_ref[...]) * dt
    ri = lax.broadcasted_iota(jnp.int32, (lc, lc), 0)
    ci = lax.broadcasted_iota(jnp.int32, (lc, lc), 1)
    lower = ri >= ci
    tril = jnp.where(lower, 1.0, 0.0)
    acs = jnp.dot(tril, adt, precision=lax.Precision.HIGHEST, preferred_element_type=F32)
    acs_t = acs.T
    dt_t = dt.T
    tot_t = acs_t[:, lc - 1:lc]
    w_t = jnp.exp(tot_t - acs_t) * dt_t
    etot_t = jnp.exp(tot_t)
    src_t = acs_t - jnp.log(dt_t)

    lane = lax.broadcasted_iota(jnp.int32, (lc, LANES), 1)
    left = lane < SSD_HEAD_DIM
    zero16 = jnp.zeros((lc, LANES), BF16)

    for g in range(SSD_GROUPS):
        b0 = SSD_INNER + g * SSD_STATE
        c0 = SSD_INNER + SSD_GROUPS * SSD_STATE + g * SSD_STATE
        cg = act[:, c0:c0 + SSD_STATE]
        bg_t = act[:, b0:b0 + SSD_STATE].T
        cg16 = cg.astype(BF16)
        bgt16 = bg_t.astype(BF16)
        cbg16 = _dot(cg16, bgt16).astype(BF16)
        for pr in range(SSD_HEADS_PER_GROUP // 2):
            h0 = g * SSD_HEADS_PER_GROUP + 2 * pr
            x0 = h0 * SSD_HEAD_DIM
            xpair = act[:, x0:x0 + LANES]
            rtpair = rt_ref[:, x0:x0 + LANES]
            x16 = xpair.astype(BF16)
            r16 = rtpair.astype(BF16)
            xl, xr = jnp.where(left, x16, zero16), jnp.where(left, zero16, x16)
            rl, rr = jnp.where(left, r16, zero16), jnp.where(left, zero16, r16)
            mh, ech, wh = [], [], []
            for h in (h0, h0 + 1):
                bc = jnp.broadcast_to(acs[:, h:h + 1], (lc, lc))
                seg = jnp.where(lower, bc - src_t[h:h + 1, :], -jnp.inf)
                mh.append(cbg16 * jnp.exp(seg).astype(BF16))
                ech.append(cg16 * jnp.exp(bc).astype(BF16))
                wh.append(bgt16 * jnp.broadcast_to(w_t[h:h + 1, :], (SSD_STATE, lc)).astype(BF16))
            ypair = _dot(jnp.concatenate(mh + ech, axis=1),
                         jnp.concatenate([xl, xr, rl, rr], axis=0))
            spair = _dot(jnp.concatenate(wh, axis=1), jnp.concatenate([xl, xr], axis=0))
            e0 = jnp.broadcast_to(etot_t[h0:h0 + 1, :], (1, LANES))
            e1 = jnp.broadcast_to(etot_t[h0 + 1:h0 + 2, :], (1, LANES))
            rt_ref[:, x0:x0 + LANES] = rtpair * jnp.where(left[0:1], e0, e1) + spair
            yp = ypair + dexp_ref[:, x0:x0 + LANES] * xpair
            zc = x0 % PROJ_PIECE
            zpair = z_refs[x0 // PROJ_PIECE][rows, zc:zc + LANES].astype(F32)
            ybuf_ref[:, x0:x0 + LANES] = yp * _silu(zpair)

    for g in range(SSD_GROUPS):
        g0 = g * SSD_GROUP_W
        yg = ybuf_ref[:, g0:g0 + SSD_GROUP_W]
        y_ref[rows, g0:g0 + SSD_GROUP_W] = _rms(
            yg, ng_ref[:, g0:g0 + SSD_GROUP_W], SUBLN_EPS).astype(y_ref.dtype)


def _ssd(proj, dt_raw, conv_w, conv_b, dt_bias, a_log, d_exp, norm_g, batch, seq):
    t = batch * seq
    lc = SSD_CHUNK
    tr = SSD_SUB * lc
    nc = seq // tr
    row = lambda b, c: b * nc + c
    rows_e = SSD_HIST + lc
    col = jnp.arange(SSD_CONV * rows_e)
    src = SSD_HIST + jnp.arange(lc)[:, None] - (SSD_CONV - 1) + col[None, :] // rows_e
    shift = (col[None, :] % rows_e == src).astype(BF16)

    def pieces(off, width):
        return [pl.BlockSpec((tr, PROJ_PIECE), functools.partial(
            lambda b, c, blk: (row(b, c), blk), blk=(off + p0) // PROJ_PIECE))
            for p0 in range(0, width, PROJ_PIECE)]

    n_pieces = (SSD_INNER + SSD_CONV_DIM) // PROJ_PIECE
    return pl.pallas_call(
        _ssd_body,
        grid=(batch, nc),
        in_specs=pieces(PROJ_Z_OFF, SSD_INNER) + pieces(PROJ_XBC_OFF, SSD_CONV_DIM) + [
            pl.BlockSpec((tr, LANES), lambda b, c: (row(b, c), 0)),
            pl.BlockSpec((lc, SSD_CONV * rows_e), lambda b, c: (0, 0)),
            pl.BlockSpec((SSD_CONV, SSD_CONV_DIM), lambda b, c: (0, 0)),
            pl.BlockSpec((1, SSD_CONV_DIM), lambda b, c: (0, 0)),
            pl.BlockSpec((1, LANES), lambda b, c: (0, 0)),
            pl.BlockSpec((1, LANES), lambda b, c: (0, 0)),
            pl.BlockSpec((1, SSD_INNER), lambda b, c: (0, 0)),
            pl.BlockSpec((1, SSD_INNER), lambda b, c: (0, 0)),
        ],
        out_specs=pl.BlockSpec((tr, SSD_INNER), lambda b, c: (row(b, c), 0)),
        out_shape=jax.ShapeDtypeStruct((t, SSD_INNER), BF16),
        scratch_shapes=[
            pltpu.VMEM((SSD_HIST + lc, SSD_CONV_DIM), BF16),
            pltpu.VMEM((SSD_STATE, SSD_INNER), F32),
            pltpu.VMEM((lc, SSD_INNER), F32),
        ],
        compiler_params=pltpu.CompilerParams(
            dimension_semantics=("parallel", "arbitrary"), vmem_limit_bytes=VMEM_LIMIT),
        name="ssd",
    )(*([proj] * n_pieces), dt_raw, shift, conv_w, conv_b, dt_bias, a_log, d_exp, norm_g)


def _mix_xattn_body(ao_ref, ys_ref, gl_ref, bg_ref, x_ref, wa_ref, ws_ref, wm_ref, mpost_ref,
                    xpre_ref, wq_ref, kv_ref, wo_ref, xpost_ref, o_ref):
    attn_out = _dot(ao_ref[...], wa_ref[...])
    ssd_out = _dot(ys_ref[...], ws_ref[...])
    gates = jax.nn.sigmoid(gl_ref[...].astype(F32) + bg_ref[...])
    mixed = gates[:, :D_MODEL] * attn_out + gates[:, D_MODEL:] * ssd_out
    mixed = _dot(mixed, wm_ref[...])
    x = x_ref[...] + _rms(mixed, mpost_ref[...], NORM_EPS)

    hq = _rms(x, xpre_ref[...], NORM_EPS)
    qx = (_dot(hq, wq_ref[...]) * (XA_HEAD_DIM ** -0.5)).astype(BF16)
    heads = []
    for h in range(XA_HEADS):
        c0 = h * XA_HEAD_DIM
        s = _dot_nt(qx[:, c0:c0 + XA_HEAD_DIM], kv_ref[:, c0:c0 + XA_HEAD_DIM])
        e = jnp.exp(s - jnp.max(s, axis=-1, keepdims=True))
        denom = jnp.sum(e, axis=-1, keepdims=True)
        oh = _dot(e.astype(BF16), kv_ref[:, D_MODEL + c0:D_MODEL + c0 + XA_HEAD_DIM])
        heads.append(oh / denom)
    xo = _dot(jnp.concatenate(heads, axis=1), wo_ref[...])
    o_ref[...] = x + _rms(xo, xpost_ref[...], NORM_EPS)


def _mix_xattn(attn_o, y_ssd, proj, b_gate, x, w_attn, w_ssd, w_mix, mix_post_g,
               xa_pre_g, w_q, kv, w_o, xa_post_g, seq):
    t = x.shape[0]
    tm = MIX_TM
    gw = N_BRANCH * D_MODEL
    per_batch = seq // tm
    row_tile = lambda w: pl.BlockSpec((tm, w), lambda i: (i, 0))
    return pl.pallas_call(
        _mix_xattn_body,
        grid=(t // tm,),
        in_specs=[
            row_tile(DA_V),
            row_tile(SSD_INNER),
            pl.BlockSpec((tm, gw), lambda i: (i, PROJ_GATE_OFF // gw)),
            _const_spec((1, gw)),
            row_tile(D_MODEL),
            _const_spec((DA_V, D_MODEL)),
            _const_spec((SSD_INNER, D_MODEL)),
            _const_spec((D_MODEL, D_MODEL)),
            _const_spec((1, D_MODEL)),
            _const_spec((1, D_MODEL)),
            _const_spec((D_MODEL, D_MODEL)),
            pl.BlockSpec((MEM_LEN, 2 * D_MODEL), lambda i: (i // per_batch, 0)),
            _const_spec((D_MODEL, D_MODEL)),
            _const_spec((1, D_MODEL)),
        ],
        out_specs=row_tile(D_MODEL),
        out_shape=jax.ShapeDtypeStruct((t, D_MODEL), F32),
        compiler_params=pltpu.CompilerParams(
            dimension_semantics=("parallel",), vmem_limit_bytes=VMEM_LIMIT),
        name="mix_xattn",
    )(attn_o, y_ssd, proj, b_gate, x, w_attn, w_ssd, w_mix, mix_post_g,
      xa_pre_g, w_q, kv, w_o, xa_post_g)


def _row(v):
    return v.reshape(1, -1)


def _pad_lanes(v):
    return jnp.pad(v, ((0, 0), (0, LANES - v.shape[1])))


def _layer(x, mem, layer_idx, batch, seq,
           ffn1_pre_g, ffn1_post_g, ffn1_w_gu, ffn1_w_down,
           mix_pre_g, mix_post_g, w_in, b_gate,
           da_lambda_q1, da_lambda_k1, da_lambda_q2, da_lambda_k2, da_subln_g,
           ssd_conv_w, ssd_conv_b, ssd_dt_bias, ssd_A_log, ssd_D, ssd_norm_g,
           w_branch_attn, w_branch_ssd, w_mix_out,
           xa_pre_g, xa_post_g, mem_norm_g, xa_w_q, xa_w_kv, xa_w_o,
           ffn2_pre_g, ffn2_post_g, ffn2_w_gu, ffn2_w_down):
    x = _ffn(x, _row(ffn1_pre_g), ffn1_w_gu, ffn1_w_down, _row(ffn1_post_g))

    q_scale = DA_HEAD_DIM ** -0.5 * math.log2(math.e)
    proj, dt_raw = _in_proj(x, _row(mix_pre_g), w_in.T, q_scale)

    lam_init = 0.8 - 0.6 * math.exp(-0.3 * layer_idx)
    lam_params = jnp.stack([da_lambda_q1, da_lambda_k1, da_lambda_q2, da_lambda_k2])
    attn_o = _diff_attn(proj, lam_params, _row(da_subln_g), lam_init, batch, seq)

    y_ssd = _ssd(proj, dt_raw, ssd_conv_w, _row(ssd_conv_b), _pad_lanes(_row(ssd_dt_bias)),
                 _pad_lanes(_row(ssd_A_log)), _row(jnp.repeat(ssd_D, SSD_HEAD_DIM)),
                 _row(ssd_norm_g), batch, seq)

    mem2 = mem.reshape(batch * MEM_LEN, D_MODEL)
    kv = _norm_proj(mem2, _row(mem_norm_g), xa_w_kv, MEM_LEN)
    x = _mix_xattn(attn_o, y_ssd, proj, _row(b_gate), x, w_branch_attn, w_branch_ssd, w_mix_out,
                   _row(mix_post_g), _row(xa_pre_g), xa_w_q, kv, xa_w_o, _row(xa_post_g), seq)

    x = _ffn(x, _row(ffn2_pre_g), ffn2_w_gu, ffn2_w_down, _row(ffn2_post_g))
    return x


def kernel(x, mem, ffn1_pre_g, ffn1_post_g, ffn1_w_gu, ffn1_w_down, mix_pre_g, mix_post_g, w_in, b_gate, da_lambda_q1, da_lambda_k1, da_lambda_q2, da_lambda_k2, da_subln_g, ssd_conv_w, ssd_conv_b, ssd_dt_bias, ssd_A_log, ssd_D, ssd_norm_g, w_branch_attn, w_branch_ssd, w_mix_out, xa_pre_g, xa_post_g, mem_norm_g, xa_w_q, xa_w_kv, xa_w_o, ffn2_pre_g, ffn2_post_g, ffn2_w_gu, ffn2_w_down):
    batch, seq, d = x.shape
    params = (ffn1_pre_g, ffn1_post_g, ffn1_w_gu, ffn1_w_down, mix_pre_g, mix_post_g, w_in, b_gate,
              da_lambda_q1, da_lambda_k1, da_lambda_q2, da_lambda_k2, da_subln_g,
              ssd_conv_w, ssd_conv_b, ssd_dt_bias, ssd_A_log, ssd_D, ssd_norm_g,
              w_branch_attn, w_branch_ssd, w_mix_out,
              xa_pre_g, xa_post_g, mem_norm_g, xa_w_q, xa_w_kv, xa_w_o,
              ffn2_pre_g, ffn2_post_g, ffn2_w_gu, ffn2_w_down)
    h = x.reshape(batch * seq, d)
    for layer in range(ffn1_pre_g.shape[0]):
        h = _layer(h, mem, layer, batch, seq, *[p[layer] for p in params])
    return h.reshape(batch, seq, d)
```
